```python
import jax, jax.numpy as jnp
from jax import lax
import numpy as np

D_MODEL = 1024
BATCH = 8
SEQ = 4096
DEPTH = 2

CHUNK = 64
QBLOCK = 128
MEM_LEN = 256
EPS = 1e-6
FOX_HEADS = 4
FOX_HD = 64
GLA_HEADS = 4
GLA_DK = 64
GLA_DV = 128
GLA_GATE_RANK = 16
GLA_TAU = 16.0
MLA_HEADS = 4
MLA_Q_RANK = 256
MLA_KV_RANK = 128
MLA_NOPE = 64
MLA_ROPE = 32
MLA_VD = 64
ROPE_BASE = 10000.0
XA_HEADS = 4
XA_HD = 128
D_FF = 4 * D_MODEL
N_BRANCH = 3

FOX_W = FOX_HEADS * FOX_HD
GLA_K_W = GLA_HEADS * GLA_DK
GLA_V_W = GLA_HEADS * GLA_DV
MLA_QK_HD = MLA_NOPE + MLA_ROPE
MLA_W = MLA_HEADS * MLA_VD
XA_W = XA_HEADS * XA_HD

IN_SIZES = (FOX_W, FOX_W, FOX_W, FOX_HEADS,
            GLA_K_W, GLA_K_W, GLA_V_W, GLA_GATE_RANK, GLA_V_W,
            MLA_Q_RANK, MLA_KV_RANK, MLA_ROPE,
            N_BRANCH * D_MODEL)
N_IN = sum(IN_SIZES)

kernel_name = 'hybrid_fox_gla_mla_gated_encoder'

F32 = jnp.float32


def rms_norm(x, g):
    xf = x.astype(F32)
    y = xf * lax.rsqrt(jnp.mean(xf * xf, axis=-1, keepdims=True) + EPS)
    return (y * g.astype(F32)).astype(x.dtype)


def split_heads(z, n):
    b, s, w = z.shape
    return z.reshape(b, s, n, w // n).transpose(0, 2, 1, 3)


def merge_heads(z):
    b, h, s, d = z.shape
    return z.transpose(0, 2, 1, 3).reshape(b, s, h * d)


def split_cols(z, sizes):
    out, start = [], 0
    for n in sizes:
        out.append(z[..., start:start + n])
        start += n
    return out


def rope(x, pos):
    half = x.shape[-1] // 2
    inv = ROPE_BASE ** (-jnp.arange(half, dtype=F32) / half)
    ang = pos.astype(F32)[:, None] * inv[None, :]
    cos, sin = jnp.cos(ang), jnp.sin(ang)
    xf = x.astype(F32)
    x1, x2 = xf[..., :half], xf[..., half:]
    return jnp.concatenate([x1 * cos - x2 * sin, x2 * cos + x1 * sin], axis=-1).astype(x.dtype)


def block_sweep_attention(q, k, v, scale, chunk_causal, log_decay=None):
    b, h, s, dk = q.shape
    nb = s // QBLOCK
    qb = q.reshape(b, h, nb, QBLOCK, dk).transpose(2, 0, 1, 3, 4)
    kpos = jnp.arange(s)
    idx = jnp.arange(nb)

    def one_block(args):
        i, qi = args[0], args[1]
        qpos = i * QBLOCK + jnp.arange(QBLOCK)
        logits = jnp.einsum('bhqd,bhkd->bhqk', qi, k, preferred_element_type=F32) * scale
        if log_decay is not None:
            logits = logits + args[2][..., :, None] - log_decay[:, :, None, :]
        limit = ((qpos // CHUNK) + 1) * CHUNK if chunk_causal else qpos + 1
        mask = kpos[None, :] < limit[:, None]
        p = jax.nn.softmax(jnp.where(mask, logits, -jnp.inf), axis=-1)
        return jnp.einsum('bhqk,bhkd->bhqd', p.astype(v.dtype), v)

    if log_decay is None:
        xs = (idx, qb)
    else:
        db = log_decay.reshape(b, h, nb, QBLOCK).transpose(2, 0, 1, 3)
        xs = (idx, qb, db)
    out = lax.map(one_block, xs)
    return out.transpose(1, 2, 0, 3, 4).reshape(b, h, s, v.shape[-1])


def fox_branch(q, k, v, f_logit, b_f):
    log_f = jax.nn.log_sigmoid(f_logit.astype(F32) + b_f.astype(F32))
    cum = jnp.cumsum(log_f, axis=1).transpose(0, 2, 1)
    o = block_sweep_attention(split_heads(q, FOX_HEADS), split_heads(k, FOX_HEADS),
                              split_heads(v, FOX_HEADS), FOX_HD ** -0.5,
                              chunk_causal=False, log_decay=cum)
    return merge_heads(o)


def gla_branch(q, k, v, g_low, r, w_gate, b_gate, g_out):
    b, s, _ = q.shape
    nc = s // CHUNK
    dt = q.dtype
    log_a = jax.nn.log_sigmoid((g_low @ w_gate + b_gate).astype(F32)) / GLA_TAU

    def chunks(z, d):
        return split_heads(z, GLA_HEADS).reshape(b, GLA_HEADS, nc, CHUNK, d)

    qc = chunks(q, GLA_DK).astype(F32) * (GLA_DK ** -0.5)
    kc = chunks(k, GLA_DK).astype(F32)
    vc = chunks(v, GLA_DV).astype(F32)
    cum = jnp.cumsum(chunks(log_a, GLA_DK), axis=3)
    end = cum[:, :, :, -1:, :]
    k_dec = kc * jnp.exp(end - cum)
    u = jnp.einsum('bhcld,bhcle->cbhde', k_dec, vc)
    a = jnp.exp(end[:, :, :, 0, :]).transpose(2, 0, 1, 3)

    def step(state, inp):
        u_c, a_c = inp
        state = a_c[..., None] * state + u_c
        return state, state

    _, states = lax.scan(step, jnp.zeros((b, GLA_HEADS, GLA_DK, GLA_DV), F32), (u, a))
    o = jnp.einsum('bhcld,cbhde->bhcle', qc, states).reshape(b, GLA_HEADS, s, GLA_DV)
    o = merge_heads(rms_norm(o, g_out)).astype(dt)
    return o * jax.nn.silu(r)


def mla_branch(c_q, c_kv, k_rope_in, g_q, w_uq, g_kv, w_ukv, pos):
    q = split_heads(rms_norm(c_q, g_q) @ w_uq, MLA_HEADS)
    kv = split_heads(rms_norm(c_kv, g_kv) @ w_ukv, MLA_HEADS)
    q_nope, q_rope = q[..., :MLA_NOPE], q[..., MLA_NOPE:]
    k_nope, v = kv[..., :MLA_NOPE], kv[..., MLA_NOPE:]
    k_rope = rope(k_rope_in, pos)[:, None]
    qh = jnp.concatenate([q_nope, rope(q_rope, pos)], axis=-1)
    kh = jnp.concatenate([k_nope, jnp.broadcast_to(k_rope, k_nope.shape[:-1] + (MLA_ROPE,))], axis=-1)
    o = block_sweep_attention(qh, kh, v, MLA_QK_HD ** -0.5, chunk_causal=True)
    return merge_heads(o)


def memory_cross_attention(h, m, w_xq, w_xkv, w_xo):
    q = split_heads(h @ w_xq, XA_HEADS)
    k, v = jnp.split(m @ w_xkv, 2, axis=-1)
    k, v = split_heads(k, XA_HEADS), split_heads(v, XA_HEADS)
    logits = jnp.einsum('bhqd,bhkd->bhqk', q, k, preferred_element_type=F32) * (XA_HD ** -0.5)
    p = jax.nn.softmax(logits, axis=-1)
    o = jnp.einsum('bhqk,bhkd->bhqd', p.astype(v.dtype), v)
    return merge_heads(o) @ w_xo


def setup_inputs(seed: int = 0) -> dict:
    key = jax.random.key(seed)
    ks = jax.random.split(key, 32)

    def dense(k, shape, fan_in):
        return jax.random.normal(k, shape, F32) * (fan_in ** -0.5)

    def gain(k, shape):
        return 1.0 + 0.02 * jax.random.normal(k, shape, F32)

    def bias(k, shape, scale):
        return scale * jax.random.normal(k, shape, F32)

    L, D = DEPTH, D_MODEL
    return {
        'x': jax.random.normal(ks[0], (BATCH, SEQ, D), F32),
        'mem': jax.random.normal(ks[1], (BATCH, MEM_LEN, D), F32),
        'g_mix': gain(ks[2], (L, D)),
        'w_in': dense(ks[3], (L, D, N_IN), D),
        'b_fox_forget': bias(ks[4], (L, FOX_HEADS), 0.1),
        'w_gla_gate': dense(ks[5], (L, GLA_GATE_RANK, GLA_K_W), GLA_GATE_RANK),
        'b_gla_gate': bias(ks[6], (L, GLA_K_W), 0.1),
        'g_gla_out': gain(ks[7], (L, GLA_DV)),
        'g_mla_q': gain(ks[8], (L, MLA_Q_RANK)),
        'w_mla_uq': dense(ks[9], (L, MLA_Q_RANK, MLA_HEADS * MLA_QK_HD), MLA_Q_RANK),
        'g_mla_kv': gain(ks[10], (L, MLA_KV_RANK)),
        'w_mla_ukv': dense(ks[11], (L, MLA_KV_RANK, MLA_HEADS * (MLA_NOPE + MLA_VD)), MLA_KV_RANK),
        'b_branch_gate': bias(ks[12], (L, N_BRANCH * D), 0.1),
        'w_up_fox': dense(ks[13], (L, FOX_W, D), FOX_W),
        'w_up_gla': dense(ks[14], (L, GLA_V_W, D), GLA_V_W),
        'w_up_mla': dense(ks[15], (L, MLA_W, D), MLA_W),
        'w_out': dense(ks[16], (L, D, D), D),
        'g_xa': gain(ks[17], (L, D)),
        'g_mem': gain(ks[18], (L, D)),
        'w_xq': dense(ks[19], (L, D, XA_W), D),
        'w_xkv': dense(ks[20], (L, D, 2 * XA_W), D),
        'w_xo': dense(ks[21], (L, XA_W, D), XA_W),
        'g_mlp': gain(ks[22], (L, D)),
        'w_mlp1': dense(ks[23], (L, D, D_FF), D),
        'w_mlp2': dense(ks[24], (L, D_FF, D), D_FF),
        'g_final': gain(ks[25], (D,)),
    }


def reference(x, mem, g_mix, w_in, b_fox_forget, w_gla_gate, b_gla_gate, g_gla_out,
              g_mla_q, w_mla_uq, g_mla_kv, w_mla_ukv, b_branch_gate,
              w_up_fox, w_up_gla, w_up_mla, w_out, g_xa, g_mem, w_xq, w_xkv, w_xo,
              g_mlp, w_mlp1, w_mlp2, g_final):
    b, s, d = x.shape
    pos = jnp.arange(s)
    for l in range(DEPTH):
        h = rms_norm(x, g_mix[l])
        z = h @ w_in[l]
        (fq, fk, fv, ff, gq, gk, gv, glow, gr, mq, mkv, mkr, zg) = split_cols(z, IN_SIZES)
        o_fox = fox_branch(fq, fk, fv, ff, b_fox_forget[l])
        o_gla = gla_branch(gq, gk, gv, glow, gr, w_gla_gate[l], b_gla_gate[l], g_gla_out[l])
        o_mla = mla_branch(mq, mkv, mkr, g_mla_q[l], w_mla_uq[l], g_mla_kv[l], w_mla_ukv[l], pos)
        gates = jax.nn.sigmoid((zg + b_branch_gate[l]).astype(F32)).astype(x.dtype)
        gates = gates.reshape(b, s, N_BRANCH, d)
        y = (gates[:, :, 0] * (o_fox @ w_up_fox[l])
             + gates[:, :, 1] * (o_gla @ w_up_gla[l])
             + gates[:, :, 2] * (o_mla @ w_up_mla[l]))
        x = x + y @ w_out[l]
        x = x + memory_cross_attention(rms_norm(x, g_xa[l]), rms_norm(mem, g_mem[l]),
                                       w_xq[l], w_xkv[l], w_xo[l])
        hm = rms_norm(x, g_mlp[l])
        x = x + jnp.square(jax.nn.relu(hm @ w_mlp1[l])) @ w_mlp2[l]
    return rms_norm(x, g_final)
```

```python
import functools

import numpy as np
import jax
import jax.numpy as jnp
from jax import lax
from jax.experimental import pallas as pl
from jax.experimental.pallas import tpu as pltpu

F32 = jnp.float32
BF16 = jnp.bfloat16

CHUNK = 64
EPS = 1e-6
FOX_HEADS, FOX_HD = 4, 64
GLA_HEADS, GLA_DK, GLA_DV, GLA_GATE_RANK, GLA_TAU = 4, 64, 128, 16, 16.0
MLA_HEADS, MLA_Q_RANK, MLA_KV_RANK, MLA_NOPE, MLA_ROPE, MLA_VD = 4, 256, 128, 64, 32, 64
ROPE_BASE = 10000.0
XA_HEADS, XA_HD = 4, 128
N_BRANCH = 3

LANES = 128
VMEM_LIMIT = 56 * 1024 * 1024

N_HEADS = 4
SLOTS = N_HEADS * LANES
MASK_VALUE = -1e30

DEC_LANE = FOX_HD
MISC_FF, MISC_GLOW, MISC_ROPE = 0, 4, 64


def _params(sem):
    return pltpu.CompilerParams(dimension_semantics=sem, vmem_limit_bytes=VMEM_LIMIT)


def _dot(a, b):
    return jnp.dot(a, b, preferred_element_type=F32)


def _dot_nt(a, b):
    return lax.dot_general(a, b, (((1,), (1,)), ((), ())), preferred_element_type=F32)


def _dot_tn(a, b):
    return lax.dot_general(a, b, (((0,), (0,)), ((), ())), preferred_element_type=F32)


def _rms(x, g):
    y = x * lax.rsqrt(jnp.mean(x * x, axis=-1, keepdims=True) + EPS)
    return y * g


def _log_sigmoid(x):
    return -(jnp.maximum(-x, 0.0) + jnp.log1p(jnp.exp(-jnp.abs(x))))


def _split3(x):
    p1 = x.astype(BF16)
    r1 = x - p1.astype(F32)
    p2 = r1.astype(BF16)
    r2 = r1 - p2.astype(F32)
    return p1, p2, r2.astype(BF16)


def _lower_tri(n):
    r = lax.broadcasted_iota(jnp.int32, (n, n), 0)
    c = lax.broadcasted_iota(jnp.int32, (n, n), 1)
    return (r >= c).astype(BF16)


def _cumsum_rows(tri, pieces):
    return _dot(tri, pieces[0]) + _dot(tri, pieces[1]) + _dot(tri, pieces[2])


def _div_pow2(x, n):
    assert n & (n - 1) == 0, n
    return jnp.right_shift(x, n.bit_length() - 1)


def _full(shape):
    return pl.BlockSpec(shape, lambda *_: (0,) * len(shape))


def _inproj_kernel(x_ref, g_ref, w_ref, *out_refs, cols):
    h = _rms(x_ref[...], g_ref[...]).astype(BF16)
    for o_ref, (c0, cw) in zip(out_refs, cols):
        o_ref[...] = _dot(h, w_ref[:, c0:c0 + cw]).astype(o_ref.dtype)


def _inproj(x2d, g, w, outs, tm):
    t, d = x2d.shape
    cols, c0 = [], 0
    for cw, _ in outs:
        cols.append((c0, cw))
        c0 += cw
    return pl.pallas_call(
        functools.partial(_inproj_kernel, cols=tuple(cols)),
        grid=(t // tm,),
        in_specs=[pl.BlockSpec((tm, d), lambda i: (i, 0)), _full((1, d)), _full(w.shape)],
        out_specs=[pl.BlockSpec((tm, cw), lambda i: (i, 0)) for cw, _ in outs],
        out_shape=[jax.ShapeDtypeStruct((t, cw), dt) for cw, dt in outs],
        compiler_params=_params(("parallel",)),
        name="inproj",
    )(x2d, g, w)


def _fox_prep_kernel(fq_ref, fk_ref, misc_ref, bf_ref, qo_ref, ko_ref, carry_ref, *, ts):
    @pl.when(pl.program_id(1) == 0)
    def _():
        carry_ref[...] = jnp.zeros_like(carry_ref)

    lane = lax.broadcasted_iota(jnp.int32, (ts, LANES), 1)
    z = jnp.where(lane < FOX_HEADS, misc_ref[...] + bf_ref[...], 0.0)
    lf = jnp.where(lane < FOX_HEADS, _log_sigmoid(z), 0.0)
    cum = _cumsum_rows(_lower_tri(ts), _split3(lf)) + carry_ref[0:1, :]
    carry_ref[...] = jnp.broadcast_to(cum[ts - 1:ts, :], carry_ref.shape)
    d = DEC_LANE
    for h in range(FOX_HEADS):
        sl = slice(h * LANES, (h + 1) * LANES)
        ch = jnp.broadcast_to(cum[:, h:h + 1], (ts, LANES))
        c1, c2, c3 = (p.astype(F32) for p in _split3(ch))
        q = fq_ref[:, sl].astype(F32) * (FOX_HD ** -0.5)
        k = fk_ref[:, sl].astype(F32)
        one = jnp.ones_like(q)
        qa = jnp.where(lane == d, c1, jnp.where(lane == d + 1, c2, jnp.where(lane == d + 2, c3,
             jnp.where((lane >= d + 3) & (lane < d + 6), one, q))))
        ka = jnp.where((lane >= d) & (lane < d + 3), one, jnp.where(lane == d + 3, -c1,
             jnp.where(lane == d + 4, -c2, jnp.where(lane == d + 5, -c3, k))))
        qo_ref[:, sl] = qa.astype(BF16)
        ko_ref[:, sl] = ka.astype(BF16)


def _fox_prep(fq, fk, misc, bf, ts):
    b, s, w = fq.shape
    spec = pl.BlockSpec((None, ts, w), lambda bi, i: (bi, i, 0))
    return pl.pallas_call(
        functools.partial(_fox_prep_kernel, ts=ts),
        grid=(b, s // ts),
        in_specs=[spec, spec, pl.BlockSpec((None, ts, LANES), lambda bi, i: (bi, i, 0)),
                  _full((1, LANES))],
        out_specs=[spec, spec],
        out_shape=[jax.ShapeDtypeStruct(fq.shape, BF16)] * 2,
        scratch_shapes=[pltpu.VMEM((8, LANES), F32)],
        compiler_params=_params(("parallel", "arbitrary")),
        name="fox_prep",
    )(fq, fk, misc, bf)


def _mla_prep_kernel(cq_ref, ckv_ref, misc_ref, miscb_ref, gq_ref, gkv_ref, wqa_ref, wqb_ref,
                     wk_ref, wv_ref, ct_ref, st_ref, q_out, k_out, v_out, *, ts):
    hq = _rms(cq_ref[...], gq_ref[...]).astype(BF16)
    hkv = _rms(ckv_ref[...], gkv_ref[...]).astype(BF16)
    ct = ct_ref[...]
    st = st_ref[...]
    lane = lax.broadcasted_iota(jnp.int32, (ts, LANES), 1)
    in_rope = (lane >= MISC_ROPE) & (lane < MISC_ROPE + MLA_ROPE)
    k_rope = jnp.where(in_rope, misc_ref[...] * ct + miscb_ref[...] * st, 0.0)
    qa = _dot(hq, wqa_ref[...])
    qb = _dot(hq, wqb_ref[...])
    kk = _dot(hkv, wk_ref[...])
    v_out[...] = _dot(hkv, wv_ref[...]).astype(BF16)
    for h in range(MLA_HEADS):
        sl = slice(h * LANES, (h + 1) * LANES)
        q_out[:, sl] = (qa[:, sl] * ct + qb[:, sl] * st).astype(BF16)
        k_out[:, sl] = (kk[:, sl] + k_rope).astype(BF16)


def _mla_prep(cq, ckv, misc, miscb, gq, gkv, wqa, wqb, wk, wv, ct, st, ts, seq):
    t = cq.shape[0]
    nblk = seq // ts
    row = lambda w: pl.BlockSpec((ts, w), lambda i: (i, 0))
    tab = pl.BlockSpec((ts, LANES), lambda i: (i % nblk, 0))
    return pl.pallas_call(
        functools.partial(_mla_prep_kernel, ts=ts),
        grid=(t // ts,),
        in_specs=[row(MLA_Q_RANK), row(MLA_KV_RANK), row(LANES), row(LANES),
                  _full(gq.shape), _full(gkv.shape), _full(wqa.shape), _full(wqb.shape),
                  _full(wk.shape), _full(wv.shape), tab, tab],
        out_specs=[row(SLOTS)] * 3,
        out_shape=[jax.ShapeDtypeStruct((t, SLOTS), BF16)] * 3,
        compiler_params=_params(("parallel",)),
        name="mla_prep",
    )(cq, ckv, misc, miscb, gq, gkv, wqa, wqb, wk, wv, ct, st)


def _gla_kernel(gq_ref, gk_ref, gv_ref, gr_ref, misc_ref, wg_ref, bg_ref, go_ref, o_ref,
                state_ref, *, n_chunks):
    kw = GLA_HEADS * GLA_DK
    vw = GLA_HEADS * GLA_DV

    @pl.when(pl.program_id(1) == 0)
    def _():
        state_ref[...] = jnp.zeros_like(state_ref)

    rb = _div_pow2(lax.broadcasted_iota(jnp.int32, (kw, vw), 0), GLA_DK)
    cb = _div_pow2(lax.broadcasted_iota(jnp.int32, (kw, vw), 1), GLA_DV)
    same_head = rb == cb
    tri = _lower_tri(CHUNK)
    ones = jnp.ones((CHUNK, LANES), BF16)
    g_out = go_ref[...]
    for c in range(n_chunks):
        rs = slice(c * CHUNK, (c + 1) * CHUNK)
        gate = _dot(misc_ref[rs, :].astype(BF16), wg_ref[...]) + bg_ref[...]
        pieces = _split3(_log_sigmoid(gate) * (1.0 / GLA_TAU))
        cum = _cumsum_rows(tri, pieces)
        end = cum[CHUNK - 1:CHUNK, :]
        k_dec = (gk_ref[rs, :] * jnp.exp(end - cum)).astype(BF16)
        end_col = _dot_tn(pieces[0], ones) + _dot_tn(pieces[1], ones) + _dot_tn(pieces[2], ones)
        a = jnp.tile(jnp.exp(end_col), (1, vw // LANES))
        u = _dot_tn(k_dec, gv_ref[rs, :])
        state = a * state_ref[...] + jnp.where(same_head, u, 0.0)
        state_ref[...] = state
        q = (gq_ref[rs, :].astype(F32) * (GLA_DK ** -0.5)).astype(BF16)
        o = _dot(q, state.astype(BF16))
        for h in range(GLA_HEADS):
            sl = slice(h * GLA_DV, (h + 1) * GLA_DV)
            r = gr_ref[rs, sl]
            y = _rms(o[:, sl], g_out)
            o_ref[rs, sl] = (y * (r * jax.nn.sigmoid(r))).astype(BF16)


def _gla(gq, gk, gv, gr, misc, wg, bg, go, tc):
    b, s, _ = gq.shape
    spec = lambda w: pl.BlockSpec((None, tc, w), lambda bi, i: (bi, i, 0))
    kw, vw = GLA_HEADS * GLA_DK, GLA_HEADS * GLA_DV
    return pl.pallas_call(
        functools.partial(_gla_kernel, n_chunks=tc // CHUNK),
        grid=(b, s // tc),
        in_specs=[spec(kw), spec(kw), spec(vw), spec(vw), spec(LANES),
                  _full(wg.shape), _full(bg.shape), _full(go.shape)],
        out_specs=spec(vw),
        out_shape=jax.ShapeDtypeStruct((b, s, vw), BF16),
        scratch_shapes=[pltpu.VMEM((kw, vw), F32)],
        compiler_params=_params(("parallel", "arbitrary")),
        name="gla",
    )(gq, gk, gv, gr, misc, wg, bg, go)


def _flash_kernel(qi_ref, kj_ref, q_ref, k_ref, v_ref, o_ref, m_ref, l_ref, acc_ref,
                  *, mode, scale, tq, tk, nk):
    step = pl.program_id(1)
    i = qi_ref[step]
    j = kj_ref[step]
    last_j = ((i + 1) * tq - 1) // tk if mode != "full" else nk - 1

    @pl.when(j == 0)
    def _():
        m_ref[...] = jnp.full_like(m_ref, MASK_VALUE)
        l_ref[...] = jnp.zeros_like(l_ref)
        acc_ref[...] = jnp.zeros_like(acc_ref)

    def body(masked):
        if masked:
            t_pos = i * tq + lax.broadcasted_iota(jnp.int32, (tq, tk), 0)
            s_pos = j * tk + lax.broadcasted_iota(jnp.int32, (tq, tk), 1)
            if mode == "causal":
                keep = s_pos <= t_pos
            else:
                keep = _div_pow2(s_pos, CHUNK) <= _div_pow2(t_pos, CHUNK)
        for h in range(N_HEADS):
            sl = slice(h * LANES, (h + 1) * LANES)
            s = _dot_nt(q_ref[:, sl], k_ref[:, sl])
            if scale != 1.0:
                s = s * scale
            if masked:
                s = jnp.where(keep, s, MASK_VALUE)
            m_prev = m_ref[h]
            m_new = jnp.maximum(m_prev, jnp.max(s, axis=-1, keepdims=True))
            alpha = jnp.exp(m_prev - m_new)
            p = jnp.exp(s - m_new[:, 0:1])
            l_ref[h] = alpha * l_ref[h] + jnp.sum(p, axis=-1, keepdims=True)
            acc_ref[:, sl] = alpha * acc_ref[:, sl] + _dot(p.astype(BF16), v_ref[:, sl])
            m_ref[h] = m_new

    if mode == "full":
        body(False)
    else:
        if mode == "causal":
            all_valid = (j + 1) * tk - 1 <= i * tq
        else:
            all_valid = ((j + 1) * tk - 1) // CHUNK <= (i * tq) // CHUNK

        @pl.when(all_valid)
        def _():
            body(False)

        @pl.when(jnp.logical_not(all_valid))
        def _():
            body(True)

    @pl.when(j == last_j)
    def _():
        for h in range(N_HEADS):
            sl = slice(h * LANES, (h + 1) * LANES)
            o_ref[:, sl] = (acc_ref[:, sl] / l_ref[h]).astype(o_ref.dtype)


def _flash(q, k, v, *, mode, scale, tq, tk):
    b, s, w = q.shape
    sk = k.shape[1]
    nq, nk = s // tq, sk // tk
    steps = [(i, j) for i in range(nq) for j in range(nk)
             if mode == "full" or j * tk < (i + 1) * tq]
    qi = jnp.asarray(np.array([p[0] for p in steps], np.int32))
    kj = jnp.asarray(np.array([p[1] for p in steps], np.int32))
    grid_spec = pltpu.PrefetchScalarGridSpec(
        num_scalar_prefetch=2,
        grid=(b, len(steps)),
        in_specs=[pl.BlockSpec((None, tq, w), lambda bi, st, qi, kj: (bi, qi[st], 0)),
                  pl.BlockSpec((None, tk, w), lambda bi, st, qi, kj: (bi, kj[st], 0)),
                  pl.BlockSpec((None, tk, w), lambda bi, st, qi, kj: (bi, kj[st], 0))],
        out_specs=pl.BlockSpec((None, tq, w), lambda bi, st, qi, kj: (bi, qi[st], 0)),
        scratch_shapes=[pltpu.VMEM((N_HEADS, tq, LANES), F32),
                        pltpu.VMEM((N_HEADS, tq, LANES), F32),
                        pltpu.VMEM((tq, w), F32)],
    )
    return pl.pallas_call(
        functools.partial(_flash_kernel, mode=mode, scale=scale, tq=tq, tk=tk, nk=nk),
        grid_spec=grid_spec,
        out_shape=jax.ShapeDtypeStruct(q.shape, BF16),
        compiler_params=_params(("parallel", "arbitrary")),
        name="flash_" + mode,
    )(qi, kj, q, k, v)


def _merge_kernel(x_ref, of_ref, og_ref, om_ref, g_ref, wz_ref, bz_ref, wf_ref, wgl_ref, wm_ref,
                  wo_ref, o_ref):
    x = x_ref[...]
    d = x.shape[-1]
    h = _rms(x, g_ref[...]).astype(BF16)
    y = None
    for br, (b_ref, w_ref) in enumerate(((of_ref, wf_ref), (og_ref, wgl_ref), (om_ref, wm_ref))):
        cs = slice(br * d, (br + 1) * d)
        gate = jax.nn.sigmoid(_dot(h, wz_ref[:, cs]) + bz_ref[:, cs])
        term = gate * _dot(b_ref[...], w_ref[...])
        y = term if y is None else y + term
    o_ref[...] = x + _dot(y.astype(BF16), wo_ref[...])


def _merge(x2d, of, og, om, g, wz, bz, wf, wgl, wm, wo, tm):
    t, d = x2d.shape
    row = lambda w: pl.BlockSpec((tm, w), lambda i: (i, 0))
    return pl.pallas_call(
        _merge_kernel,
        grid=(t // tm,),
        in_specs=[row(d), row(of.shape[1]), row(og.shape[1]), row(om.shape[1]), _full(g.shape),
                  _full(wz.shape), _full(bz.shape), _full(wf.shape), _full(wgl.shape),
                  _full(wm.shape), _full(wo.shape)],
        out_specs=row(d),
        out_shape=jax.ShapeDtypeStruct((t, d), F32),
        compiler_params=_params(("parallel",)),
        name="merge",
    )(x2d, of, og, om, g, wz, bz, wf, wgl, wm, wo)


def _memkv_kernel(m_ref, g_ref, w_ref, k_ref, v_ref):
    h = _rms(m_ref[...], g_ref[...]).astype(BF16)
    kv = _dot(h, w_ref[...])
    w = k_ref.shape[-1]
    k_ref[...] = kv[:, :w].astype(BF16)
    v_ref[...] = kv[:, w:].astype(BF16)


def _memkv(mem2d, g, w, tm):
    t, d = mem2d.shape
    xw = w.shape[1] // 2
    return pl.pallas_call(
        _memkv_kernel,
        grid=(t // tm,),
        in_specs=[pl.BlockSpec((tm, d), lambda i: (i, 0)), _full(g.shape), _full(w.shape)],
        out_specs=[pl.BlockSpec((tm, xw), lambda i: (i, 0))] * 2,
        out_shape=[jax.ShapeDtypeStruct((t, xw), BF16)] * 2,
        compiler_params=_params(("parallel",)),
        name="memkv",
    )(mem2d, g, w)


def _xattn_kernel(x_ref, k_ref, v_ref, g_ref, wq_ref, wo_ref, o_ref):
    x = x_ref[...]
    h = _rms(x, g_ref[...]).astype(BF16)
    q = _dot(h, wq_ref[...]).astype(BF16)
    outs = []
    for hd in range(XA_HEADS):
        sl = slice(hd * XA_HD, (hd + 1) * XA_HD)
        s = _dot_nt(q[:, sl], k_ref[:, sl]) * (XA_HD ** -0.5)
        e = jnp.exp(s - jnp.max(s, axis=-1, keepdims=True))
        p = e / jnp.sum(e, axis=-1, keepdims=True)
        outs.append(_dot(p.astype(BF16), v_ref[:, sl]).astype(BF16))
    o = jnp.concatenate(outs, axis=-1)
    o_ref[...] = x + _dot(o, wo_ref[...])


def _xattn(x, k, v, g, wq, wo, tm):
    b, s, d = x.shape
    m, xw = k.shape[1], k.shape[2]
    return pl.pallas_call(
        _xattn_kernel,
        grid=(b, s // tm),
        in_specs=[pl.BlockSpec((None, tm, d), lambda bi, i: (bi, i, 0)),
                  pl.BlockSpec((None, m, xw), lambda bi, i: (bi, 0, 0)),
                  pl.BlockSpec((None, m, xw), lambda bi, i: (bi, 0, 0)),
                  _full(g.shape), _full(wq.shape), _full(wo.shape)],
        out_specs=pl.BlockSpec((None, tm, d), lambda bi, i: (bi, i, 0)),
        out_shape=jax.ShapeDtypeStruct(x.shape, F32),
        compiler_params=_params(("parallel", "parallel")),
        name="xattn",
    )(x, k, v, g, wq, wo)


def _mlp_kernel(x_ref, g_ref, w1_ref, w2_ref, gf_ref, o_ref, *, ff_tile, final_norm):
    x = x_ref[...]
    h = _rms(x, g_ref[...]).astype(BF16)
    acc = x
    for f0 in range(0, w1_ref.shape[1], ff_tile):
        a = jnp.square(jnp.maximum(_dot(h, w1_ref[:, f0:f0 + ff_tile]), 0.0))
        acc = acc + _dot(a.astype(BF16), w2_ref[f0:f0 + ff_tile, :])
    o_ref[...] = _rms(acc, gf_ref[...]) if final_norm else acc


def _mlp(x2d, g, w1, w2, gf, tm, final_norm):
    t, d = x2d.shape
    row = pl.BlockSpec((tm, d), lambda i: (i, 0))
    return pl.pallas_call(
        functools.partial(_mlp_kernel, ff_tile=min(1024, w1.shape[1]), final_norm=final_norm),
        grid=(t // tm,),
        in_specs=[row, _full(g.shape), _full(w1.shape), _full(w2.shape), _full(gf.shape)],
        out_specs=row,
        out_shape=jax.ShapeDtypeStruct((t, d), F32),
        compiler_params=_params(("parallel",)),
        name="mlp",
    )(x2d, g, w1, w2, gf)


def _slot_cols(w, heads, hd):
    k = w.shape[0]
    w = w.reshape(k, heads, hd)
    return jnp.pad(w, ((0, 0), (0, 0), (0, LANES - hd))).reshape(k, heads * LANES)


def _slot_rows(w, heads, hd):
    n = w.shape[1]
    w = w.reshape(heads, hd, n)
    return jnp.pad(w, ((0, 0), (0, LANES - hd), (0, 0))).reshape(heads * LANES, n)


def _place(cols_by_lane, k):
    out = jnp.zeros((k, LANES), F32)
    for lane0, w in cols_by_lane:
        out = lax.dynamic_update_slice(out, w, (0, lane0))
    return out


def _inproj_weights(w):
    fw, kw, vw = FOX_HEADS * FOX_HD, GLA_HEADS * GLA_DK, GLA_HEADS * GLA_DV
    sizes = (fw, fw, fw, FOX_HEADS, kw, kw, vw, GLA_GATE_RANK, vw,
             MLA_Q_RANK, MLA_KV_RANK, MLA_ROPE, N_BRANCH * w.shape[0])
    parts, c0 = [], 0
    for n in sizes:
        parts.append(w[:, c0:c0 + n])
        c0 += n
    fq, fk, fv, ff, gq, gk, gv, glow, gr, mq, mkv, mkr, zg = parts
    k = w.shape[0]
    half = MLA_ROPE // 2
    misc = _place(((MISC_FF, ff), (MISC_GLOW, glow), (MISC_ROPE, mkr)), k)
    miscb = _place(((MISC_ROPE, mkr[:, half:]), (MISC_ROPE + half, mkr[:, :half])), k)
    w1 = jnp.concatenate([_slot_cols(fq, FOX_HEADS, FOX_HD), _slot_cols(fk, FOX_HEADS, FOX_HD),
                          _slot_cols(fv, FOX_HEADS, FOX_HD), gq, gk, gv, gr, misc, miscb, mq, mkv],
                         axis=1).astype(BF16)
    outs = [(SLOTS, BF16), (SLOTS, BF16), (SLOTS, BF16), (kw, BF16), (kw, F32), (vw, BF16),
            (vw, F32), (LANES, F32), (LANES, F32), (MLA_Q_RANK, F32), (MLA_KV_RANK, F32)]
    return w1, outs, zg.astype(BF16)


def _mla_weights(w_uq, w_ukv):
    half = MLA_ROPE // 2
    qk = MLA_NOPE + MLA_ROPE
    r = w_uq.shape[0]
    wq = w_uq.reshape(r, MLA_HEADS, qk)
    nope, x1, x2 = wq[..., :MLA_NOPE], wq[..., MLA_NOPE:MLA_NOPE + half], wq[..., MLA_NOPE + half:]
    pad = jnp.zeros((r, MLA_HEADS, LANES - qk), F32)
    wqa = jnp.concatenate([nope, x1, x2, pad], axis=-1).reshape(r, SLOTS)
    wqb = jnp.concatenate([jnp.zeros_like(nope), x2, x1, pad], axis=-1).reshape(r, SLOTS)
    rk = w_ukv.shape[0]
    wkv = w_ukv.reshape(rk, MLA_HEADS, MLA_NOPE + MLA_VD)
    wk = _slot_cols(wkv[..., :MLA_NOPE].reshape(rk, -1), MLA_HEADS, MLA_NOPE)
    wv = _slot_cols(wkv[..., MLA_NOPE:].reshape(rk, -1), MLA_HEADS, MLA_VD)
    return wqa.astype(BF16), wqb.astype(BF16), wk.astype(BF16), wv.astype(BF16)


def _rope_tables(seq):
    half = MLA_ROPE // 2
    inv = ROPE_BASE ** (-jnp.arange(half, dtype=F32) / half)
    ang = jnp.arange(seq).astype(F32)[:, None] * inv[None, :]
    cos, sin = jnp.cos(ang), jnp.sin(ang)
    one = jnp.ones((seq, MLA_NOPE), F32)
    zero = jnp.zeros((seq, LANES - MLA_NOPE - MLA_ROPE), F32)
    ct = jnp.concatenate([one, cos, cos, zero], axis=1)
    st = jnp.concatenate([jnp.zeros_like(one), -sin, sin, zero], axis=1)
    return ct, st


def _tile(n, pref):
    t = min(n, pref)
    assert n % t == 0, (n, t)
    return t


def kernel(x, mem, g_mix, w_in, b_fox_forget, w_gla_gate, b_gla_gate, g_gla_out, g_mla_q, w_mla_uq, g_mla_kv, w_mla_ukv, b_branch_gate, w_up_fox, w_up_gla, w_up_mla, w_out, g_xa, g_mem, w_xq, w_xkv, w_xo, g_mlp, w_mlp1, w_mlp2, g_final):
    b, s, d = x.shape
    depth = w_in.shape[0]
    t = b * s
    mlen = mem.shape[1]
    tm = _tile(t, 512)
    ts = _tile(s, 256)
    tq = _tile(s, 512)
    ct, st = _rope_tables(s)
    x2d = x.reshape(t, d)
    mem2d = mem.reshape(b * mlen, d)
    row = lambda v: v.reshape(1, -1)
    for l in range(depth):
        w1, outs, wz = _inproj_weights(w_in[l])
        fq, fk, fv, gq, gk, gv, gr, misc, miscb, cq, ckv = _inproj(x2d, row(g_mix[l]), w1, outs, tm)
        as3 = lambda a: a.reshape(b, s, a.shape[-1])
        bf = jnp.pad(b_fox_forget[l], (0, LANES - FOX_HEADS)).reshape(1, LANES)
        fqa, fka = _fox_prep(as3(fq), as3(fk), as3(misc), bf, ts)
        o_fox = _flash(fqa, fka, as3(fv), mode="causal", scale=1.0, tq=tq, tk=tq)
        wqa, wqb, wk, wv = _mla_weights(w_mla_uq[l], w_mla_ukv[l])
        mq, mk, mv = _mla_prep(cq, ckv, misc, miscb, row(g_mla_q[l]), row(g_mla_kv[l]),
                               wqa, wqb, wk, wv, ct, st, ts, s)
        o_mla = _flash(as3(mq), as3(mk), as3(mv), mode="chunk",
                       scale=(MLA_NOPE + MLA_ROPE) ** -0.5, tq=tq, tk=tq)
        wg = jnp.pad(w_gla_gate[l], ((MISC_GLOW, LANES - MISC_GLOW - GLA_GATE_RANK), (0, 0))).astype(BF16)
        o_gla = _gla(as3(gq), as3(gk), as3(gv), as3(gr), as3(misc), wg, row(b_gla_gate[l]),
                     row(g_gla_out[l]), ts)
        x2d = _merge(x2d, o_fox.reshape(t, SLOTS), o_gla.reshape(t, -1), o_mla.reshape(t, SLOTS),
                     row(g_mix[l]), wz, row(b_branch_gate[l]),
                     _slot_rows(w_up_fox[l], FOX_HEADS, FOX_HD).astype(BF16),
                     w_up_gla[l].astype(BF16),
                     _slot_rows(w_up_mla[l], MLA_HEADS, MLA_VD).astype(BF16),
                     w_out[l].astype(BF16), tm)
        km, vm = _memkv(mem2d, row(g_mem[l]), w_xkv[l].astype(BF16), _tile(b * mlen, 512))
        xw = km.shape[-1]
        x2d = _xattn(x2d.reshape(b, s, d), km.reshape(b, mlen, xw), vm.reshape(b, mlen, xw),
                     row(g_xa[l]), w_xq[l].astype(BF16), w_xo[l].astype(BF16),
                     _tile(s, 512)).reshape(t, d)
        x2d = _mlp(x2d, row(g_mlp[l]), w_mlp1[l].astype(BF16), w_mlp2[l].astype(BF16),
                   row(g_final), tm, final_norm=(l == depth - 1))
    return x2d.reshape(b, s, d)
```

```python
import functools

import numpy as np
import jax
import jax.numpy as jnp
from jax import lax
from jax.experimental import pallas as pl
from jax.experimental.pallas import tpu as pltpu

F32 = jnp.float32
BF16 = jnp.bfloat16

CHUNK = 64
EPS = 1e-6
FOX_HEADS, FOX_HD = 4, 64
GLA_HEADS, GLA_DK, GLA_DV, GLA_GATE_RANK, GLA_TAU = 4, 64, 128, 16, 16.0
MLA_HEADS, MLA_Q_RANK, MLA_KV_RANK, MLA_NOPE, MLA_ROPE, MLA_VD = 4, 256, 128, 64, 32, 64
ROPE_BASE = 10000.0
XA_HEADS, XA_HD = 4, 128
N_BRANCH = 3

LANES = 128
MXU_DEPTH = 256
VMEM_LIMIT = 56 * 1024 * 1024

N_HEADS = 4
SLOTS = N_HEADS * LANES
MASK_VALUE = -1e30

DEC_LANE = FOX_HD
ROPE_LANE = MLA_NOPE
DEN_ROW = 64
assert DEN_ROW == FOX_HD == MLA_VD
LOG2E = 1.4426950408889634


def _params(sem):
    return pltpu.CompilerParams(dimension_semantics=sem, vmem_limit_bytes=VMEM_LIMIT)


def _dot(a, b):
    return jnp.dot(a, b, preferred_element_type=F32)


def _dot_nt(a, b):
    return lax.dot_general(a, b, (((1,), (1,)), ((), ())), preferred_element_type=F32)


def _dot_tn(a, b):
    return lax.dot_general(a, b, (((0,), (0,)), ((), ())), preferred_element_type=F32)


def _rms(x, g):
    y = x * lax.rsqrt(jnp.mean(x * x, axis=-1, keepdims=True) + EPS)
    return y * g


def _log_sigmoid(x):
    return -(jnp.maximum(-x, 0.0) + jnp.log1p(jnp.exp(-jnp.abs(x))))


def _split3(x):
    p1 = x.astype(BF16)
    r1 = x - p1.astype(F32)
    p2 = r1.astype(BF16)
    r2 = r1 - p2.astype(F32)
    return p1, p2, r2.astype(BF16)


def _lower_tri(n):
    r = lax.broadcasted_iota(jnp.int32, (n, n), 0)
    c = lax.broadcasted_iota(jnp.int32, (n, n), 1)
    return (r >= c).astype(BF16)


def _cumsum_rows(tri, pieces):
    return _dot(tri, pieces[0]) + _dot(tri, pieces[1]) + _dot(tri, pieces[2])


def _div_pow2(x, n):
    assert n & (n - 1) == 0, n
    return jnp.right_shift(x, n.bit_length() - 1)


def _full(shape):
    return pl.BlockSpec(shape, lambda *_: (0,) * len(shape))


def _inproj_kernel(x_ref, g_ref, w_ref, *out_refs, cols):
    h = _rms(x_ref[...], g_ref[...]).astype(BF16)
    for o_ref, (c0, cw, transposed) in zip(out_refs, cols):
        y = _dot(h, w_ref[:, c0:c0 + cw])
        o_ref[...] = (y.T if transposed else y).astype(o_ref.dtype)


def _inproj(x2d, g, w, outs, tm):
    t, d = x2d.shape
    cols, c0 = [], 0
    for cw, _, tr in outs:
        cols.append((c0, cw, tr))
        c0 += cw
    return pl.pallas_call(
        functools.partial(_inproj_kernel, cols=tuple(cols)),
        grid=(t // tm,),
        in_specs=[pl.BlockSpec((tm, d), lambda i: (i, 0)), _full((1, d)), _full(w.shape)],
        out_specs=[pl.BlockSpec((cw, tm), lambda i: (0, i)) if tr else pl.BlockSpec((tm, cw), lambda i: (i, 0))
                   for cw, _, tr in outs],
        out_shape=[jax.ShapeDtypeStruct((cw, t) if tr else (t, cw), dt) for cw, dt, tr in outs],
        compiler_params=_params(("parallel",)),
        name="inproj",
    )(x2d, g, w)


def _fox_prep_kernel(fq_ref, fk_ref, fv_ref, ff_ref, bf_ref, qo_ref, ko_ref, vo_ref, carry_ref, *, ts):
    @pl.when(pl.program_id(1) == 0)
    def _():
        carry_ref[...] = jnp.zeros_like(carry_ref)

    lane = lax.broadcasted_iota(jnp.int32, (ts, LANES), 1)
    z = jnp.where(lane < FOX_HEADS, ff_ref[...] + bf_ref[...], 0.0)
    lf = jnp.where(lane < FOX_HEADS, _log_sigmoid(z), 0.0)
    cum = _cumsum_rows(_lower_tri(ts), _split3(lf)) + carry_ref[0:1, :]
    carry_ref[...] = jnp.broadcast_to(cum[ts - 1:ts, :], carry_ref.shape)
    d = DEC_LANE
    for h in range(FOX_HEADS):
        sl = slice(h * LANES, (h + 1) * LANES)
        ch = jnp.broadcast_to(cum[:, h:h + 1], (ts, LANES))
        c1, c2, c3 = (p.astype(F32) for p in _split3(ch))
        q = fq_ref[:, sl].astype(F32) * (FOX_HD ** -0.5)
        k = fk_ref[:, sl].astype(F32)
        one = jnp.ones_like(q)
        qa = jnp.where(lane == d, c1, jnp.where(lane == d + 1, c2, jnp.where(lane == d + 2, c3,
             jnp.where((lane >= d + 3) & (lane < d + 6), one, q))))
        ka = jnp.where((lane >= d) & (lane < d + 3), one, jnp.where(lane == d + 3, -c1,
             jnp.where(lane == d + 4, -c2, jnp.where(lane == d + 5, -c3, k))))
        qo_ref[h] = qa.T.astype(BF16)
        ko_ref[h] = ka.astype(BF16)
        v = jnp.where(lane == DEN_ROW, 1.0, fv_ref[:, sl].astype(F32))
        vo_ref[h] = v.T.astype(BF16)


def _head_major_specs(batch, seq, ts):
    spec_t = pl.BlockSpec((None, N_HEADS, LANES, ts), lambda bi, i: (bi, 0, 0, i))
    spec_n = pl.BlockSpec((None, N_HEADS, ts, LANES), lambda bi, i: (bi, 0, i, 0))
    shape_t = jax.ShapeDtypeStruct((batch, N_HEADS, LANES, seq), BF16)
    shape_n = jax.ShapeDtypeStruct((batch, N_HEADS, seq, LANES), BF16)
    return spec_t, spec_n, shape_t, shape_n


def _fox_prep(fq, fk, fv, ff, bf, ts):
    b, s, w = fq.shape
    spec = pl.BlockSpec((None, ts, w), lambda bi, i: (bi, i, 0))
    spec_t, spec_n, shape_t, shape_n = _head_major_specs(b, s, ts)
    return pl.pallas_call(
        functools.partial(_fox_prep_kernel, ts=ts),
        grid=(b, s // ts),
        in_specs=[spec, spec, spec, pl.BlockSpec((None, ts, LANES), lambda bi, i: (bi, i, 0)),
                  _full((1, LANES))],
        out_specs=[spec_t, spec_n, spec_t],
        out_shape=[shape_t, shape_n, shape_t],
        scratch_shapes=[pltpu.VMEM((8, LANES), F32)],
        compiler_params=_params(("parallel", "arbitrary")),
        name="fox_prep",
    )(fq, fk, fv, ff, bf)


def _mla_prep_kernel(cq_ref, ckv_ref, kr_ref, krs_ref, gq_ref, gkv_ref, wqa_ref, wqb_ref,
                     wk_ref, wv_ref, ct_ref, st_ref, q_out, k_out, v_out, *, ts):
    hq = _rms(cq_ref[...], gq_ref[...]).astype(BF16)
    hkv = _rms(ckv_ref[...], gkv_ref[...]).astype(BF16)
    ct = ct_ref[...]
    st = st_ref[...]
    lane = lax.broadcasted_iota(jnp.int32, (ts, LANES), 1)
    in_rope = (lane >= ROPE_LANE) & (lane < ROPE_LANE + MLA_ROPE)
    k_rope = jnp.where(in_rope, kr_ref[...] * ct + krs_ref[...] * st, 0.0)
    qa = _dot(hq, wqa_ref[...])
    qb = _dot(hq, wqb_ref[...])
    kk = _dot(hkv, wk_ref[...])
    vv = _dot(hkv, wv_ref[...])
    for h in range(MLA_HEADS):
        sl = slice(h * LANES, (h + 1) * LANES)
        q_out[h] = (qa[:, sl] * ct + qb[:, sl] * st).T.astype(BF16)
        k_out[h] = (kk[:, sl] + k_rope).astype(BF16)
        v_out[h] = jnp.where(lane == DEN_ROW, 1.0, vv[:, sl]).T.astype(BF16)


def _mla_prep(cq, ckv, kr, krs, gq, gkv, wqa, wqb, wk, wv, ct, st, ts, batch, seq):
    nblk = seq // ts
    row = lambda w: pl.BlockSpec((ts, w), lambda bi, i: (bi * nblk + i, 0))
    tab = pl.BlockSpec((ts, LANES), lambda bi, i: (i, 0))
    full = lambda a: pl.BlockSpec(a.shape, lambda bi, i: (0,) * a.ndim)
    spec_t, spec_n, shape_t, shape_n = _head_major_specs(batch, seq, ts)
    return pl.pallas_call(
        functools.partial(_mla_prep_kernel, ts=ts),
        grid=(batch, nblk),
        in_specs=[row(MLA_Q_RANK), row(MLA_KV_RANK), row(LANES), row(LANES),
                  full(gq), full(gkv), full(wqa), full(wqb), full(wk), full(wv), tab, tab],
        out_specs=[spec_t, spec_n, spec_t],
        out_shape=[shape_t, shape_n, shape_t],
        compiler_params=_params(("parallel", "parallel")),
        name="mla_prep",
    )(cq, ckv, kr, krs, gq, gkv, wqa, wqb, wk, wv, ct, st)


def _gla_kernel(gq_ref, gk_ref, gvt_ref, gr_ref, glow_ref, wg_ref, bg_ref, go_ref, o_ref,
                state_ref, *, tc):
    kw = GLA_HEADS * GLA_DK
    vw = GLA_HEADS * GLA_DV

    @pl.when(pl.program_id(1) == 0)
    def _():
        state_ref[...] = jnp.zeros_like(state_ref)

    same_head = (_div_pow2(lax.broadcasted_iota(jnp.int32, (vw, kw), 0), GLA_DV)
                 == _div_pow2(lax.broadcasted_iota(jnp.int32, (vw, kw), 1), GLA_DK))
    gr_rows = MXU_DEPTH
    ri = lax.broadcasted_iota(jnp.int32, (gr_rows, gr_rows), 0)
    ci = lax.broadcasted_iota(jnp.int32, (gr_rows, gr_rows), 1)
    tri = ((_div_pow2(ri, CHUNK) == _div_pow2(ci, CHUNK)) & (ri >= ci)).astype(BF16)
    chunk_of_row = _div_pow2(lax.broadcasted_iota(jnp.int32, (gr_rows, kw), 0), CHUNK)
    g_out = go_ref[...]
    for g0 in range(0, tc, gr_rows):
        gs = slice(g0, g0 + gr_rows)
        gate = _dot(glow_ref[gs, :].astype(BF16), wg_ref[...]) + bg_ref[...]
        cum = _cumsum_rows(tri, _split3(_log_sigmoid(gate) * (1.0 / GLA_TAU)))
        ends = [cum[(c + 1) * CHUNK - 1:(c + 1) * CHUNK, :] for c in range(gr_rows // CHUNK)]
        end_rows = jnp.concatenate([jnp.broadcast_to(e, (CHUNK, kw)) for e in ends], axis=0)
        k_dec = gk_ref[gs, :] * jnp.exp(end_rows - cum)
        q = (gq_ref[gs, :].astype(F32) * (GLA_DK ** -0.5)).astype(BF16)
        gvt = gvt_ref[:, gs]
        for c in range(gr_rows // CHUNK):
            rs = slice(c * CHUNK, (c + 1) * CHUNK)
            k_c = jnp.where(chunk_of_row == c, k_dec, 0.0).astype(BF16)
            u_t = _dot(gvt, k_c)
            state = jnp.exp(ends[c]) * state_ref[...] + jnp.where(same_head, u_t, 0.0)
            state_ref[...] = state
            o = _dot_nt(q[rs, :], state.astype(BF16))
            for h in range(GLA_HEADS):
                sl = slice(h * GLA_DV, (h + 1) * GLA_DV)
                r = gr_ref[g0 + c * CHUNK:g0 + (c + 1) * CHUNK, sl]
                y = _rms(o[:, sl], g_out)
                o_ref[g0 + c * CHUNK:g0 + (c + 1) * CHUNK, sl] = (y * (r * jax.nn.sigmoid(r))).astype(BF16)


def _gla(gq, gk, gvt, gr, glow, wg, bg, go, tc):
    b, s, _ = gq.shape
    nblk = s // tc
    spec = lambda w: pl.BlockSpec((None, tc, w), lambda bi, i: (bi, i, 0))
    kw, vw = GLA_HEADS * GLA_DK, GLA_HEADS * GLA_DV
    return pl.pallas_call(
        functools.partial(_gla_kernel, tc=tc),
        grid=(b, nblk),
        in_specs=[spec(kw), spec(kw), pl.BlockSpec((vw, tc), lambda bi, i: (0, bi * nblk + i)),
                  spec(vw), spec(LANES), _full(wg.shape), _full(bg.shape), _full(go.shape)],
        out_specs=spec(vw),
        out_shape=jax.ShapeDtypeStruct((b, s, vw), BF16),
        scratch_shapes=[pltpu.VMEM((vw, kw), F32)],
        compiler_params=_params(("parallel", "arbitrary")),
        name="gla",
    )(gq, gk, gvt, gr, glow, wg, bg, go)


QK_ROWS = 128
PV_ROWS = 256


def _flash_kernel(qi_ref, kj_ref, qt_ref, k_ref, vt_ref, o_ref, m_ref, alpha_ref, acc_ref, s_ref,
                  *, mode, scale, tq, tk):
    step = pl.program_id(1)
    i = qi_ref[step]
    j = kj_ref[step]
    last_j = ((i + 1) * tq - 1) // tk

    @pl.when(j == 0)
    def _():
        m_ref[...] = jnp.full_like(m_ref, MASK_VALUE)
        acc_ref[...] = jnp.zeros_like(acc_ref)

    def scores(h, masked):
        buf = h & 1
        m_prev = m_ref[h]
        m_new = m_prev
        for r in range(0, tk, QK_ROWS):
            s = _dot(k_ref[h, r:r + QK_ROWS, :], qt_ref[h]) * (scale * LOG2E)
            if masked:
                s_pos = j * tk + r + lax.broadcasted_iota(jnp.int32, (QK_ROWS, tq), 0)
                t_pos = i * tq + lax.broadcasted_iota(jnp.int32, (QK_ROWS, tq), 1)
                if mode == "causal":
                    keep = s_pos <= t_pos
                else:
                    keep = _div_pow2(s_pos, CHUNK) <= _div_pow2(t_pos, CHUNK)
                s = jnp.where(keep, s, MASK_VALUE)
            s_ref[buf, r:r + QK_ROWS, :] = s
            m_new = jnp.maximum(m_new, jnp.max(s, axis=0, keepdims=True))
        alpha_ref[h] = jnp.exp2(m_prev - m_new)
        m_ref[h] = m_new

    def values(h):
        buf = h & 1
        m_new = m_ref[h]
        pv = None
        for r in range(0, tk, PV_ROWS):
            p = jnp.exp2(s_ref[buf, r:r + PV_ROWS, :] - m_new).astype(BF16)
            d = _dot(vt_ref[h, :, r:r + PV_ROWS], p)
            pv = d if pv is None else pv + d
        acc_ref[h] = alpha_ref[h] * acc_ref[h] + pv

    def body(masked):
        scores(0, masked)
        for h in range(N_HEADS - 1):
            scores(h + 1, masked)
            values(h)
        values(N_HEADS - 1)

    if mode == "causal":
        all_valid = (j + 1) * tk - 1 <= i * tq
    else:
        all_valid = ((j + 1) * tk - 1) // CHUNK <= (i * tq) // CHUNK

    @pl.when(all_valid)
    def _():
        body(False)

    @pl.when(jnp.logical_not(all_valid))
    def _():
        body(True)

    @pl.when(j == last_j)
    def _():
        for h in range(N_HEADS):
            acc = acc_ref[h]
            o_ref[:, h * LANES:(h + 1) * LANES] = (acc / acc[DEN_ROW:DEN_ROW + 1, :]).T.astype(o_ref.dtype)


def _flash(qt, k, vt, *, mode, scale, tq, tk):
    b, nh, s, w = k.shape
    nq, nk = s // tq, s // tk
    assert tk % QK_ROWS == 0 and tk % PV_ROWS == 0
    steps = [(i, j) for i in range(nq) for j in range(nk) if j * tk < (i + 1) * tq]
    qi = jnp.asarray(np.array([p[0] for p in steps], np.int32))
    kj = jnp.asarray(np.array([p[1] for p in steps], np.int32))
    grid_spec = pltpu.PrefetchScalarGridSpec(
        num_scalar_prefetch=2,
        grid=(b, len(steps)),
        in_specs=[pl.BlockSpec((None, nh, w, tq), lambda bi, st, qi, kj: (bi, 0, 0, qi[st])),
                  pl.BlockSpec((None, nh, tk, w), lambda bi, st, qi, kj: (bi, 0, kj[st], 0)),
                  pl.BlockSpec((None, nh, w, tk), lambda bi, st, qi, kj: (bi, 0, 0, kj[st]))],
        out_specs=pl.BlockSpec((None, tq, nh * w), lambda bi, st, qi, kj: (bi, qi[st], 0)),
        scratch_shapes=[pltpu.VMEM((nh, 1, tq), F32),
                        pltpu.VMEM((nh, 1, tq), F32),
                        pltpu.VMEM((nh, w, tq), F32),
                        pltpu.VMEM((2, tk, tq), F32)],
    )
    return pl.pallas_call(
        functools.partial(_flash_kernel, mode=mode, scale=scale, tq=tq, tk=tk),
        grid_spec=grid_spec,
        out_shape=jax.ShapeDtypeStruct((b, s, nh * w), BF16),
        compiler_params=_params(("parallel", "arbitrary")),
        name="flash_" + mode,
    )(qi, kj, qt, k, vt)


def _merge_kernel(x_ref, of_ref, og_ref, om_ref, g_ref, wz_ref, bz_ref, wf_ref, wgl_ref, wm_ref,
                  wo_ref, o_ref):
    x = x_ref[...]
    d = x.shape[-1]
    h = _rms(x, g_ref[...]).astype(BF16)
    y = None
    for br, (b_ref, w_ref) in enumerate(((of_ref, wf_ref), (og_ref, wgl_ref), (om_ref, wm_ref))):
        cs = slice(br * d, (br + 1) * d)
        gate = jax.nn.sigmoid(_dot(h, wz_ref[:, cs]) + bz_ref[:, cs])
        term = gate * _dot(b_ref[...], w_ref[...])
        y = term if y is None else y + term
    o_ref[...] = x + _dot(y.astype(BF16), wo_ref[...])


def _merge(x2d, of, og, om, g, wz, bz, wf, wgl, wm, wo, tm):
    t, d = x2d.shape
    row = lambda w: pl.BlockSpec((tm, w), lambda i: (i, 0))
    return pl.pallas_call(
        _merge_kernel,
        grid=(t // tm,),
        in_specs=[row(d), row(of.shape[1]), row(og.shape[1]), row(om.shape[1]), _full(g.shape),
                  _full(wz.shape), _full(bz.shape), _full(wf.shape), _full(wgl.shape),
                  _full(wm.shape), _full(wo.shape)],
        out_specs=row(d),
        out_shape=jax.ShapeDtypeStruct((t, d), F32),
        compiler_params=_params(("parallel",)),
        name="merge",
    )(x2d, of, og, om, g, wz, bz, wf, wgl, wm, wo)


def _memkv_kernel(m_ref, g_ref, w_ref, k_ref, v_ref):
    h = _rms(m_ref[...], g_ref[...]).astype(BF16)
    kv = _dot(h, w_ref[...])
    w = k_ref.shape[-1]
    k_ref[...] = kv[:, :w].astype(BF16)
    v_ref[...] = kv[:, w:].astype(BF16)


def _memkv(mem2d, g, w, tm):
    t, d = mem2d.shape
    xw = w.shape[1] // 2
    return pl.pallas_call(
        _memkv_kernel,
        grid=(t // tm,),
        in_specs=[pl.BlockSpec((tm, d), lambda i: (i, 0)), _full(g.shape), _full(w.shape)],
        out_specs=[pl.BlockSpec((tm, xw), lambda i: (i, 0))] * 2,
        out_shape=[jax.ShapeDtypeStruct((t, xw), BF16)] * 2,
        compiler_params=_params(("parallel",)),
        name="memkv",
    )(mem2d, g, w)


def _xattn_kernel(x_ref, k_ref, v_ref, g_ref, wq_ref, wo_ref, o_ref):
    x = x_ref[...]
    h = _rms(x, g_ref[...]).astype(BF16)
    q = _dot(h, wq_ref[...]).astype(BF16)
    outs = []
    for hd in range(XA_HEADS):
        sl = slice(hd * XA_HD, (hd + 1) * XA_HD)
        s = _dot_nt(q[:, sl], k_ref[:, sl]) * (XA_HD ** -0.5)
        e = jnp.exp(s - jnp.max(s, axis=-1, keepdims=True))
        p = e / jnp.sum(e, axis=-1, keepdims=True)
        outs.append(_dot(p.astype(BF16), v_ref[:, sl]).astype(BF16))
    o = jnp.concatenate(outs, axis=-1)
    o_ref[...] = x + _dot(o, wo_ref[...])


def _xattn(x, k, v, g, wq, wo, tm):
    b, s, d = x.shape
    m, xw = k.shape[1], k.shape[2]
    return pl.pallas_call(
        _xattn_kernel,
        grid=(b, s // tm),
        in_specs=[pl.BlockSpec((None, tm, d), lambda bi, i: (bi, i, 0)),
                  pl.BlockSpec((None, m, xw), lambda bi, i: (bi, 0, 0)),
                  pl.BlockSpec((None, m, xw), lambda bi, i: (bi, 0, 0)),
                  _full(g.shape), _full(wq.shape), _full(wo.shape)],
        out_specs=pl.BlockSpec((None, tm, d), lambda bi, i: (bi, i, 0)),
        out_shape=jax.ShapeDtypeStruct(x.shape, F32),
        compiler_params=_params(("parallel", "parallel")),
        name="xattn",
    )(x, k, v, g, wq, wo)


def _mlp_kernel(x_ref, g_ref, w1_ref, w2_ref, gf_ref, o_ref, *, ff_tile, final_norm):
    x = x_ref[...]
    h = _rms(x, g_ref[...]).astype(BF16)
    acc = x
    for f0 in range(0, w1_ref.shape[1], ff_tile):
        a = jnp.square(jnp.maximum(_dot(h, w1_ref[:, f0:f0 + ff_tile]), 0.0))
        acc = acc + _dot(a.astype(BF16), w2_ref[f0:f0 + ff_tile, :])
    o_ref[...] = _rms(acc, gf_ref[...]) if final_norm else acc


def _mlp(x2d, g, w1, w2, gf, tm, final_norm):
    t, d = x2d.shape
    row = pl.BlockSpec((tm, d), lambda i: (i, 0))
    return pl.pallas_call(
        functools.partial(_mlp_kernel, ff_tile=min(1024, w1.shape[1]), final_norm=final_norm),
        grid=(t // tm,),
        in_specs=[row, _full(g.shape), _full(w1.shape), _full(w2.shape), _full(gf.shape)],
        out_specs=row,
        out_shape=jax.ShapeDtypeStruct((t, d), F32),
        compiler_params=_params(("parallel",)),
        name="mlp",
    )(x2d, g, w1, w2, gf)


def _slot_cols(w, heads, hd):
    k = w.shape[0]
    w = w.reshape(k, heads, hd)
    return jnp.pad(w, ((0, 0), (0, 0), (0, LANES - hd))).reshape(k, heads * LANES)


def _slot_rows(w, heads, hd):
    n = w.shape[1]
    w = w.reshape(heads, hd, n)
    return jnp.pad(w, ((0, 0), (0, LANES - hd), (0, 0))).reshape(heads * LANES, n)


REPACK_PIECES = 2


def _repack_kernel(blk_ref, par_ref, a_ref, b_ref, o_ref, *, n_cols):
    i = pl.program_id(0)
    rows = a_ref.shape[0]
    col = blk_ref[i] * LANES + lax.broadcasted_iota(jnp.int32, (rows, 2 * LANES), 1)
    window = jnp.concatenate([a_ref[...], b_ref[...]], axis=1)
    window = jnp.where(col < n_cols, window, 0.0).astype(BF16)
    r = lax.broadcasted_iota(jnp.int32, (2 * LANES, LANES), 0)
    j = lax.broadcasted_iota(jnp.int32, (2 * LANES, LANES), 1)
    sel = None
    for p in range(REPACK_PIECES):
        off = par_ref[(i * REPACK_PIECES + p) * 3]
        width = par_ref[(i * REPACK_PIECES + p) * 3 + 1]
        dst = par_ref[(i * REPACK_PIECES + p) * 3 + 2]
        hit = (r - off == j - dst) & (j >= dst) & (j < dst + width)
        sel = hit if sel is None else sel | hit
    o_ref[...] = _dot(window, sel.astype(BF16)).astype(BF16)


def _repack(w_all, layer, blocks):
    _, k, n_cols = w_all.shape
    last_blk = (n_cols - 1) // LANES
    blk, par = [], []
    for pieces in blocks:
        b0 = min(p[0] for p in pieces) // LANES
        assert len(pieces) <= REPACK_PIECES
        pieces = list(pieces) + [(b0 * LANES, 0, 0)] * (REPACK_PIECES - len(pieces))
        for src, width, dst in pieces:
            assert 0 <= src - b0 * LANES and src - b0 * LANES + width <= 2 * LANES and dst + width <= LANES
            par += [src - b0 * LANES, width, dst]
        blk.append(b0)
    grid_spec = pltpu.PrefetchScalarGridSpec(
        num_scalar_prefetch=2,
        grid=(len(blocks),),
        in_specs=[pl.BlockSpec((None, k, LANES), lambda i, blk, par: (layer, 0, blk[i])),
                  pl.BlockSpec((None, k, LANES),
                               lambda i, blk, par: (layer, 0, jnp.minimum(blk[i] + 1, last_blk)))],
        out_specs=pl.BlockSpec((k, LANES), lambda i, blk, par: (0, i)),
    )
    return pl.pallas_call(
        functools.partial(_repack_kernel, n_cols=n_cols),
        grid_spec=grid_spec,
        out_shape=jax.ShapeDtypeStruct((k, len(blocks) * LANES), BF16),
        compiler_params=_params(("arbitrary",)),
        name="repack",
    )(jnp.asarray(np.array(blk, np.int32)), jnp.asarray(np.array(par, np.int32)), w_all, w_all)


def _inproj_weights(w_all, layer):
    d = w_all.shape[1]
    fw, kw, vw = FOX_HEADS * FOX_HD, GLA_HEADS * GLA_DK, GLA_HEADS * GLA_DV
    sizes = (fw, fw, fw, FOX_HEADS, kw, kw, vw, GLA_GATE_RANK, vw,
             MLA_Q_RANK, MLA_KV_RANK, MLA_ROPE, N_BRANCH * d)
    starts = np.concatenate([[0], np.cumsum(sizes)]).tolist()
    fq, fk, fv, ff, gq, gk, gv, glow, gr, mq, mkv, mkr, zg = starts[:-1]
    half = MLA_ROPE // 2
    slots = lambda c0, heads, hd: [[(c0 + h * hd, hd, 0)] for h in range(heads)]
    dense = lambda c0, width: [[(c0 + i, LANES, 0)] for i in range(0, width, LANES)]
    blocks = (slots(fq, FOX_HEADS, FOX_HD) + slots(fk, FOX_HEADS, FOX_HD) + slots(fv, FOX_HEADS, FOX_HD)
              + dense(gq, kw) + dense(gk, kw) + dense(gv, vw) + dense(gr, vw)
              + [[(ff, FOX_HEADS, 0)], [(glow, GLA_GATE_RANK, 0)], [(mkr, MLA_ROPE, ROPE_LANE)],
                 [(mkr + half, half, ROPE_LANE), (mkr, half, ROPE_LANE + half)]]
              + dense(mq, MLA_Q_RANK) + dense(mkv, MLA_KV_RANK))
    n, tr = False, True
    outs = [(SLOTS, BF16, n), (SLOTS, BF16, n), (SLOTS, BF16, n), (kw, BF16, n), (kw, F32, n),
            (vw, BF16, tr), (vw, F32, n), (LANES, F32, n), (LANES, F32, n), (LANES, F32, n),
            (LANES, F32, n), (MLA_Q_RANK, F32, n), (MLA_KV_RANK, F32, n)]
    return _repack(w_all, layer, blocks), outs, _repack(w_all, layer, dense(zg, N_BRANCH * d))


def _mla_weights(w_uq, w_ukv):
    half = MLA_ROPE // 2
    qk = MLA_NOPE + MLA_ROPE
    r = w_uq.shape[0]
    wq = w_uq.reshape(r, MLA_HEADS, qk)
    nope, x1, x2 = wq[..., :MLA_NOPE], wq[..., MLA_NOPE:MLA_NOPE + half], wq[..., MLA_NOPE + half:]
    pad = jnp.zeros((r, MLA_HEADS, LANES - qk), F32)
    wqa = jnp.concatenate([nope, x1, x2, pad], axis=-1).reshape(r, SLOTS)
    wqb = jnp.concatenate([jnp.zeros_like(nope), x2, x1, pad], axis=-1).reshape(r, SLOTS)
    rk = w_ukv.shape[0]
    wkv = w_ukv.reshape(rk, MLA_HEADS, MLA_NOPE + MLA_VD)
    wk = _slot_cols(wkv[..., :MLA_NOPE].reshape(rk, -1), MLA_HEADS, MLA_NOPE)
    wv = _slot_cols(wkv[..., MLA_NOPE:].reshape(rk, -1), MLA_HEADS, MLA_VD)
    return wqa.astype(BF16), wqb.astype(BF16), wk.astype(BF16), wv.astype(BF16)


def _rope_tables(seq):
    half = MLA_ROPE // 2
    inv = ROPE_BASE ** (-jnp.arange(half, dtype=F32) / half)
    ang = jnp.arange(seq).astype(F32)[:, None] * inv[None, :]
    cos, sin = jnp.cos(ang), jnp.sin(ang)
    one = jnp.ones((seq, MLA_NOPE), F32)
    zero = jnp.zeros((seq, LANES - MLA_NOPE - MLA_ROPE), F32)
    ct = jnp.concatenate([one, cos, cos, zero], axis=1)
    st = jnp.concatenate([jnp.zeros_like(one), -sin, sin, zero], axis=1)
    return ct, st


def _tile(n, pref):
    t = min(n, pref)
    assert n % t == 0, (n, t)
    return t


def kernel(x, mem, g_mix, w_in, b_fox_forget, w_gla_gate, b_gla_gate, g_gla_out, g_mla_q, w_mla_uq, g_mla_kv, w_mla_ukv, b_branch_gate, w_up_fox, w_up_gla, w_up_mla, w_out, g_xa, g_mem, w_xq, w_xkv, w_xo, g_mlp, w_mlp1, w_mlp2, g_final):
    b, s, d = x.shape
    depth = w_in.shape[0]
    t = b * s
    mlen = mem.shape[1]
    tm = _tile(t, 512)
    ts = _tile(s, 256)
    tq = _tile(s, 512)
    ct, st = _rope_tables(s)
    x2d = x.reshape(t, d)
    mem2d = mem.reshape(b * mlen, d)
    row = lambda v: v.reshape(1, -1)
    for l in range(depth):
        w1, outs, wz = _inproj_weights(w_in, l)
        fq, fk, fv, gq, gk, gv, gr, ff, glow, kr, krs, cq, ckv = _inproj(x2d, row(g_mix[l]), w1, outs, tm)
        as3 = lambda a: a.reshape(b, s, a.shape[-1])
        bf = jnp.pad(b_fox_forget[l], (0, LANES - FOX_HEADS)).reshape(1, LANES)
        fqt, fka, fvt = _fox_prep(as3(fq), as3(fk), as3(fv), as3(ff), bf, ts)
        o_fox = _flash(fqt, fka, fvt, mode="causal", scale=1.0, tq=tq, tk=tq)
        wqa, wqb, wk, wv = _mla_weights(w_mla_uq[l], w_mla_ukv[l])
        mqt, mk, mvt = _mla_prep(cq, ckv, kr, krs, row(g_mla_q[l]), row(g_mla_kv[l]),
                                 wqa, wqb, wk, wv, ct, st, ts, b, s)
        o_mla = _flash(mqt, mk, mvt, mode="chunk",
                       scale=(MLA_NOPE + MLA_ROPE) ** -0.5, tq=tq, tk=tq)
        wg = jnp.pad(w_gla_gate[l], ((0, LANES - GLA_GATE_RANK), (0, 0))).astype(BF16)
        o_gla = _gla(as3(gq), as3(gk), gv, as3(gr), as3(glow), wg, row(b_gla_gate[l]),
                     row(g_gla_out[l]), _tile(s, 512))
        x2d = _merge(x2d, o_fox.reshape(t, SLOTS), o_gla.reshape(t, -1), o_mla.reshape(t, SLOTS),
                     row(g_mix[l]), wz, row(b_branch_gate[l]),
                     _slot_rows(w_up_fox[l], FOX_HEADS, FOX_HD).astype(BF16),
                     w_up_gla[l].astype(BF16),
                     _slot_rows(w_up_mla[l], MLA_HEADS, MLA_VD).astype(BF16),
                     w_out[l].astype(BF16), tm)
        km, vm = _memkv(mem2d, row(g_mem[l]), w_xkv[l].astype(BF16), _tile(b * mlen, 512))
        xw = km.shape[-1]
        x2d = _xattn(x2d.reshape(b, s, d), km.reshape(b, mlen, xw), vm.reshape(b, mlen, xw),
                     row(g_xa[l]), w_xq[l].astype(BF16), w_xo[l].astype(BF16),
                     _tile(s, 512)).reshape(t, d)
        x2d = _mlp(x2d, row(g_mlp[l]), w_mlp1[l].astype(BF16), w_mlp2[l].astype(BF16),
                   row(g_final), tm, final_norm=(l == depth - 1))
    return x2d.reshape(b, s, d)
```

```python
import functools

import numpy as np
import jax
import jax.numpy as jnp
from jax import lax
from jax.experimental import pallas as pl
from jax.experimental.pallas import tpu as pltpu

F32 = jnp.float32
BF16 = jnp.bfloat16

CHUNK = 64
EPS = 1e-6
FOX_HEADS, FOX_HD = 4, 64
GLA_HEADS, GLA_DK, GLA_DV, GLA_GATE_RANK, GLA_TAU = 4, 64, 128, 16, 16.0
MLA_HEADS, MLA_Q_RANK, MLA_KV_RANK, MLA_NOPE, MLA_ROPE, MLA_VD = 4, 256, 128, 64, 32, 64
ROPE_BASE = 10000.0
XA_HEADS, XA_HD = 4, 128
N_BRANCH = 3

LANES = 128
MXU_DEPTH = 256
VMEM_LIMIT = 56 * 1024 * 1024

N_HEADS = 4
SLOTS = N_HEADS * LANES
MASK_VALUE = -1e30

DEC_LANE = FOX_HD
ROPE_LANE = MLA_NOPE
DEN_ROW = 64
assert DEN_ROW == FOX_HD == MLA_VD
LOG2E = 1.4426950408889634


def _params(sem):
    return pltpu.CompilerParams(dimension_semantics=sem, vmem_limit_bytes=VMEM_LIMIT)


def _dot(a, b):
    return jnp.dot(a, b, preferred_element_type=F32)


def _dot_nt(a, b):
    return lax.dot_general(a, b, (((1,), (1,)), ((), ())), preferred_element_type=F32)


def _dot_tn(a, b):
    return lax.dot_general(a, b, (((0,), (0,)), ((), ())), preferred_element_type=F32)


def _rms(x, g):
    y = x * lax.rsqrt(jnp.mean(x * x, axis=-1, keepdims=True) + EPS)
    return y * g


def _log_sigmoid(x):
    return -(jnp.maximum(-x, 0.0) + jnp.log1p(jnp.exp(-jnp.abs(x))))


def _split3(x):
    p1 = x.astype(BF16)
    r1 = x - p1.astype(F32)
    p2 = r1.astype(BF16)
    r2 = r1 - p2.astype(F32)
    return p1, p2, r2.astype(BF16)


def _lower_tri(n):
    r = lax.broadcasted_iota(jnp.int32, (n, n), 0)
    c = lax.broadcasted_iota(jnp.int32, (n, n), 1)
    return (r >= c).astype(BF16)


def _cumsum_rows(tri, pieces):
    return _dot(tri, pieces[0]) + _dot(tri, pieces[1]) + _dot(tri, pieces[2])


def _div_pow2(x, n):
    assert n & (n - 1) == 0, n
    return jnp.right_shift(x, n.bit_length() - 1)


def _full(shape):
    return pl.BlockSpec(shape, lambda *_: (0,) * len(shape))


INPROJ_CHUNK = 2 * MXU_DEPTH


def _inproj_kernel(x_ref, g_ref, w_ref, *out_refs, cols):
    h = _rms(x_ref[...], g_ref[...]).astype(BF16)
    n = w_ref.shape[1]
    for c0 in range(0, n, INPROJ_CHUNK):
        c1 = min(c0 + INPROJ_CHUNK, n)
        y = _dot(h, w_ref[:, c0:c1])
        for o_ref, (o0, ow, transposed) in zip(out_refs, cols):
            lo, hi = max(o0, c0), min(o0 + ow, c1)
            if lo >= hi:
                continue
            piece = y[:, lo - c0:hi - c0]
            if transposed:
                o_ref[lo - o0:hi - o0, :] = piece.T.astype(o_ref.dtype)
            else:
                o_ref[:, lo - o0:hi - o0] = piece.astype(o_ref.dtype)


def _inproj(x2d, g, w, outs, tm):
    t, d = x2d.shape
    cols, c0 = [], 0
    for cw, _, tr in outs:
        cols.append((c0, cw, tr))
        c0 += cw
    return pl.pallas_call(
        functools.partial(_inproj_kernel, cols=tuple(cols)),
        grid=(t // tm,),
        in_specs=[pl.BlockSpec((tm, d), lambda i: (i, 0)), _full((1, d)), _full(w.shape)],
        out_specs=[pl.BlockSpec((cw, tm), lambda i: (0, i)) if tr else pl.BlockSpec((tm, cw), lambda i: (i, 0))
                   for cw, _, tr in outs],
        out_shape=[jax.ShapeDtypeStruct((cw, t) if tr else (t, cw), dt) for cw, dt, tr in outs],
        compiler_params=_params(("parallel",)),
        name="inproj",
    )(x2d, g, w)


def _fox_prep_kernel(fq_ref, fk_ref, fv_ref, ff_ref, bf_ref, eq_ref, ek_ref, qo_ref, ko_ref, vo_ref,
                     carry_ref, *, ts):
    @pl.when(pl.program_id(1) == 0)
    def _():
        carry_ref[...] = jnp.zeros_like(carry_ref)

    lane = lax.broadcasted_iota(jnp.int32, (ts, LANES), 1)
    z = jnp.where(lane < FOX_HEADS, ff_ref[...] + bf_ref[...], 0.0)
    lf = jnp.where(lane < FOX_HEADS, _log_sigmoid(z), 0.0)
    cum = _cumsum_rows(_lower_tri(ts), _split3(lf)) + carry_ref[0:1, :]
    carry_ref[...] = jnp.broadcast_to(cum[ts - 1:ts, :], carry_ref.shape)
    pieces = jnp.concatenate(_split3(cum) + (jnp.ones((ts, LANES), BF16),), axis=1)
    aug_q = _dot(pieces, eq_ref[...])
    aug_k = _dot(pieces, ek_ref[...])
    for h in range(FOX_HEADS):
        sl = slice(h * LANES, (h + 1) * LANES)
        qa = fq_ref[:, sl].astype(F32) * (FOX_HD ** -0.5) + aug_q[:, sl]
        ka = fk_ref[:, sl].astype(F32) + aug_k[:, sl]
        qo_ref[h] = qa.T.astype(BF16)
        ko_ref[h] = ka.astype(BF16)
        v = jnp.where(lane == DEN_ROW, 1.0, fv_ref[:, sl].astype(F32))
        vo_ref[h] = v.T.astype(BF16)


def _fox_placement():
    eq = np.zeros((4 * LANES, SLOTS), np.float32)
    ek = np.zeros((4 * LANES, SLOTS), np.float32)
    for h in range(FOX_HEADS):
        base = h * LANES + DEC_LANE
        for p in range(3):
            eq[p * LANES + h, base + p] = 1.0
            eq[3 * LANES, base + 3 + p] = 1.0
            ek[3 * LANES, base + p] = 1.0
            ek[p * LANES + h, base + 3 + p] = -1.0
    return jnp.asarray(eq, BF16), jnp.asarray(ek, BF16)


def _head_major_specs(batch, seq, ts):
    spec_t = pl.BlockSpec((None, N_HEADS, LANES, ts), lambda bi, i: (bi, 0, 0, i))
    spec_n = pl.BlockSpec((None, N_HEADS, ts, LANES), lambda bi, i: (bi, 0, i, 0))
    shape_t = jax.ShapeDtypeStruct((batch, N_HEADS, LANES, seq), BF16)
    shape_n = jax.ShapeDtypeStruct((batch, N_HEADS, seq, LANES), BF16)
    return spec_t, spec_n, shape_t, shape_n


def _fox_prep(fq, fk, fv, ff, bf, ts):
    b, s, w = fq.shape
    spec = pl.BlockSpec((None, ts, w), lambda bi, i: (bi, i, 0))
    spec_t, spec_n, shape_t, shape_n = _head_major_specs(b, s, ts)
    eq, ek = _fox_placement()
    return pl.pallas_call(
        functools.partial(_fox_prep_kernel, ts=ts),
        grid=(b, s // ts),
        in_specs=[spec, spec, spec, pl.BlockSpec((None, ts, LANES), lambda bi, i: (bi, i, 0)),
                  _full((1, LANES)), _full(eq.shape), _full(ek.shape)],
        out_specs=[spec_t, spec_n, spec_t],
        out_shape=[shape_t, shape_n, shape_t],
        scratch_shapes=[pltpu.VMEM((8, LANES), F32)],
        compiler_params=_params(("parallel", "arbitrary")),
        name="fox_prep",
    )(fq, fk, fv, ff, bf, eq, ek)


def _mla_prep_kernel(cq_ref, ckv_ref, kr_ref, krs_ref, gq_ref, gkv_ref, wqa_ref, wqb_ref,
                     wk_ref, wv_ref, ct_ref, st_ref, q_out, k_out, v_out, *, ts):
    hq = _rms(cq_ref[...], gq_ref[...]).astype(BF16)
    hkv = _rms(ckv_ref[...], gkv_ref[...]).astype(BF16)
    ct = ct_ref[...]
    st = st_ref[...]
    lane = lax.broadcasted_iota(jnp.int32, (ts, LANES), 1)
    in_rope = (lane >= ROPE_LANE) & (lane < ROPE_LANE + MLA_ROPE)
    k_rope = jnp.where(in_rope, kr_ref[...] * ct + krs_ref[...] * st, 0.0)
    qa = _dot(hq, wqa_ref[...])
    qb = _dot(hq, wqb_ref[...])
    kk = _dot(hkv, wk_ref[...])
    vv = _dot(hkv, wv_ref[...])
    for h in range(MLA_HEADS):
        sl = slice(h * LANES, (h + 1) * LANES)
        q_out[h] = (qa[:, sl] * ct + qb[:, sl] * st).T.astype(BF16)
        k_out[h] = (kk[:, sl] + k_rope).astype(BF16)
        v_out[h] = jnp.where(lane == DEN_ROW, 1.0, vv[:, sl]).T.astype(BF16)


def _mla_prep(cq, ckv, kr, krs, gq, gkv, wqa, wqb, wk, wv, ct, st, ts, batch, seq):
    nblk = seq // ts
    row = lambda w: pl.BlockSpec((ts, w), lambda bi, i: (bi * nblk + i, 0))
    tab = pl.BlockSpec((ts, LANES), lambda bi, i: (i, 0))
    full = lambda a: pl.BlockSpec(a.shape, lambda bi, i: (0,) * a.ndim)
    spec_t, spec_n, shape_t, shape_n = _head_major_specs(batch, seq, ts)
    return pl.pallas_call(
        functools.partial(_mla_prep_kernel, ts=ts),
        grid=(batch, nblk),
        in_specs=[row(MLA_Q_RANK), row(MLA_KV_RANK), row(LANES), row(LANES),
                  full(gq), full(gkv), full(wqa), full(wqb), full(wk), full(wv), tab, tab],
        out_specs=[spec_t, spec_n, spec_t],
        out_shape=[shape_t, shape_n, shape_t],
        compiler_params=_params(("parallel", "parallel")),
        name="mla_prep",
    )(cq, ckv, kr, krs, gq, gkv, wqa, wqb, wk, wv, ct, st)


def _gla_kernel(gq_ref, gk_ref, gvt_ref, gr_ref, glow_ref, wg_ref, bg_ref, go_ref, o_ref,
                state_ref, *, tc):
    kw = GLA_HEADS * GLA_DK
    vw = GLA_HEADS * GLA_DV

    @pl.when(pl.program_id(1) == 0)
    def _():
        state_ref[...] = jnp.zeros_like(state_ref)

    same_head = (_div_pow2(lax.broadcasted_iota(jnp.int32, (vw, kw), 0), GLA_DV)
                 == _div_pow2(lax.broadcasted_iota(jnp.int32, (vw, kw), 1), GLA_DK))
    gr_rows = MXU_DEPTH
    ri = lax.broadcasted_iota(jnp.int32, (gr_rows, gr_rows), 0)
    ci = lax.broadcasted_iota(jnp.int32, (gr_rows, gr_rows), 1)
    tri = ((_div_pow2(ri, CHUNK) == _div_pow2(ci, CHUNK)) & (ri >= ci)).astype(BF16)
    chunk_of_row = _div_pow2(lax.broadcasted_iota(jnp.int32, (gr_rows, kw), 0), CHUNK)
    g_out = go_ref[...]
    for g0 in range(0, tc, gr_rows):
        gs = slice(g0, g0 + gr_rows)
        gate = _dot(glow_ref[gs, :].astype(BF16), wg_ref[...]) + bg_ref[...]
        cum = _cumsum_rows(tri, _split3(_log_sigmoid(gate) * (1.0 / GLA_TAU)))
        ends = [cum[(c + 1) * CHUNK - 1:(c + 1) * CHUNK, :] for c in range(gr_rows // CHUNK)]
        end_rows = jnp.concatenate([jnp.broadcast_to(e, (CHUNK, kw)) for e in ends], axis=0)
        k_dec = gk_ref[gs, :] * jnp.exp(end_rows - cum)
        q = (gq_ref[gs, :].astype(F32) * (GLA_DK ** -0.5)).astype(BF16)
        gvt = gvt_ref[:, gs]
        for c in range(gr_rows // CHUNK):
            rs = slice(c * CHUNK, (c + 1) * CHUNK)
            k_c = jnp.where(chunk_of_row == c, k_dec, 0.0).astype(BF16)
            u_t = _dot(gvt, k_c)
            state = jnp.exp(ends[c]) * state_ref[...] + jnp.where(same_head, u_t, 0.0)
            state_ref[...] = state
            o = _dot_nt(q[rs, :], state.astype(BF16))
            for h in range(GLA_HEADS):
                sl = slice(h * GLA_DV, (h + 1) * GLA_DV)
                r = gr_ref[g0 + c * CHUNK:g0 + (c + 1) * CHUNK, sl]
                y = _rms(o[:, sl], g_out)
                o_ref[g0 + c * CHUNK:g0 + (c + 1) * CHUNK, sl] = (y * (r * jax.nn.sigmoid(r))).astype(BF16)


def _gla(gq, gk, gvt, gr, glow, wg, bg, go, tc):
    b, s, _ = gq.shape
    nblk = s // tc
    spec = lambda w: pl.BlockSpec((None, tc, w), lambda bi, i: (bi, i, 0))
    kw, vw = GLA_HEADS * GLA_DK, GLA_HEADS * GLA_DV
    return pl.pallas_call(
        functools.partial(_gla_kernel, tc=tc),
        grid=(b, nblk),
        in_specs=[spec(kw), spec(kw), pl.BlockSpec((vw, tc), lambda bi, i: (0, bi * nblk + i)),
                  spec(vw), spec(LANES), _full(wg.shape), _full(bg.shape), _full(go.shape)],
        out_specs=spec(vw),
        out_shape=jax.ShapeDtypeStruct((b, s, vw), BF16),
        scratch_shapes=[pltpu.VMEM((vw, kw), F32)],
        compiler_params=_params(("parallel", "arbitrary")),
        name="gla",
    )(gq, gk, gvt, gr, glow, wg, bg, go)


QK_ROWS = 128
PV_ROWS = MXU_DEPTH
V_ROWS = 80
assert V_ROWS > DEN_ROW and V_ROWS % 16 == 0


def _flash_kernel(qi_ref, kj_ref, qt_ref, k_ref, vt_ref, o_ref, m_ref, alpha_ref, acc_ref, s_ref,
                  *, mode, scale, tq, tk):
    step = pl.program_id(1)
    i = qi_ref[step]
    j = kj_ref[step]
    last_j = ((i + 1) * tq - 1) // tk
    sub = tq
    assert tk == 2 * sub

    @pl.when(j == 0)
    def _():
        m_ref[...] = jnp.full_like(m_ref, MASK_VALUE)
        acc_ref[...] = jnp.zeros_like(acc_ref)

    def scores(u, h, masked, buf):
        m_prev = m_ref[h]
        m_new = m_prev
        for r in range(0, sub, QK_ROWS):
            k0 = u * sub + r
            s = _dot(k_ref[h, k0:k0 + QK_ROWS, :], qt_ref[h]) * (scale * LOG2E)
            if masked:
                s_pos = j * tk + k0 + lax.broadcasted_iota(jnp.int32, (QK_ROWS, tq), 0)
                t_pos = i * tq + lax.broadcasted_iota(jnp.int32, (QK_ROWS, tq), 1)
                if mode == "causal":
                    keep = s_pos <= t_pos
                else:
                    keep = _div_pow2(s_pos, CHUNK) <= _div_pow2(t_pos, CHUNK)
                s = jnp.where(keep, s, MASK_VALUE)
            s_ref[buf, r:r + QK_ROWS, :] = s
            m_new = jnp.maximum(m_new, jnp.max(s, axis=0, keepdims=True))
        alpha_ref[h] = jnp.exp2(m_prev - m_new)
        m_ref[h] = m_new

    def values(u, h, buf):
        m_new = m_ref[h]
        pv = None
        for r in range(0, sub, PV_ROWS):
            p = jnp.exp2(s_ref[buf, r:r + PV_ROWS, :] - m_new).astype(BF16)
            d = _dot(vt_ref[h, 0:V_ROWS, u * sub + r:u * sub + r + PV_ROWS], p)
            pv = d if pv is None else pv + d
        acc_ref[h, 0:V_ROWS, :] = alpha_ref[h] * acc_ref[h, 0:V_ROWS, :] + pv

    def body(sub_masked):
        items = [(u, h, m) for u, m in enumerate(sub_masked) for h in range(N_HEADS)]
        scores(*items[0], 0)
        for n, (u, h, _) in enumerate(items):
            if n + 1 < len(items):
                scores(*items[n + 1], (n + 1) % 2)
            values(u, h, n % 2)

    first = j * tk
    if mode == "causal":
        full0 = first + sub - 1 <= i * tq
        full1 = first + 2 * sub - 1 <= i * tq
    else:
        full0 = (first + sub - 1) // CHUNK <= (i * tq) // CHUNK
        full1 = (first + 2 * sub - 1) // CHUNK <= (i * tq) // CHUNK

    @pl.when(full1)
    def _():
        body((False, False))

    @pl.when(full0 & jnp.logical_not(full1))
    def _():
        body((False, True))

    @pl.when(jnp.logical_not(full0))
    def _():
        body((True,))

    @pl.when(j == last_j)
    def _():
        per = LANES // DEN_ROW
        for g in range(N_HEADS // per):
            rows = [acc_ref[h, 0:DEN_ROW, :] / acc_ref[h, DEN_ROW:DEN_ROW + 1, :]
                    for h in range(g * per, (g + 1) * per)]
            o_ref[:, g * LANES:(g + 1) * LANES] = jnp.concatenate(rows, axis=0).T.astype(o_ref.dtype)


def _flash(qt, k, vt, *, mode, scale, tq, tk):
    b, nh, s, w = k.shape
    nq, nk = s // tq, s // tk
    assert tk == 2 * tq and tq % QK_ROWS == 0 and tq % PV_ROWS == 0
    steps = [(i, j) for i in range(nq) for j in range(nk) if j * tk < (i + 1) * tq]
    qi = jnp.asarray(np.array([p[0] for p in steps], np.int32))
    kj = jnp.asarray(np.array([p[1] for p in steps], np.int32))
    grid_spec = pltpu.PrefetchScalarGridSpec(
        num_scalar_prefetch=2,
        grid=(b, len(steps)),
        in_specs=[pl.BlockSpec((None, nh, w, tq), lambda bi, st, qi, kj: (bi, 0, 0, qi[st])),
                  pl.BlockSpec((None, nh, tk, w), lambda bi, st, qi, kj: (bi, 0, kj[st], 0)),
                  pl.BlockSpec((None, nh, w, tk), lambda bi, st, qi, kj: (bi, 0, 0, kj[st]))],
        out_specs=pl.BlockSpec((None, tq, nh * DEN_ROW), lambda bi, st, qi, kj: (bi, qi[st], 0)),
        scratch_shapes=[pltpu.VMEM((nh, 1, tq), F32),
                        pltpu.VMEM((nh, 1, tq), F32),
                        pltpu.VMEM((nh, w, tq), F32),
                        pltpu.VMEM((2, tq, tq), F32)],
    )
    return pl.pallas_call(
        functools.partial(_flash_kernel, mode=mode, scale=scale, tq=tq, tk=tk),
        grid_spec=grid_spec,
        out_shape=jax.ShapeDtypeStruct((b, s, nh * DEN_ROW), BF16),
        compiler_params=_params(("parallel", "arbitrary")),
        name="flash_" + mode,
    )(qi, kj, qt, k, vt)


def _merge_kernel(x_ref, of_ref, og_ref, om_ref, g_ref, wz_ref, bz_ref, wf_ref, wgl_ref, wm_ref,
                  wo_ref, o_ref):
    x = x_ref[...]
    d = x.shape[-1]
    h = _rms(x, g_ref[...]).astype(BF16)
    y = None
    for br, (b_ref, w_ref) in enumerate(((of_ref, wf_ref), (og_ref, wgl_ref), (om_ref, wm_ref))):
        cs = slice(br * d, (br + 1) * d)
        gate = jax.nn.sigmoid(_dot(h, wz_ref[:, cs]) + bz_ref[:, cs])
        term = gate * _dot(b_ref[...], w_ref[...])
        y = term if y is None else y + term
    o_ref[...] = x + _dot(y.astype(BF16), wo_ref[...])


def _merge(x2d, of, og, om, g, wz, bz, wf, wgl, wm, wo, tm):
    t, d = x2d.shape
    row = lambda w: pl.BlockSpec((tm, w), lambda i: (i, 0))
    return pl.pallas_call(
        _merge_kernel,
        grid=(t // tm,),
        in_specs=[row(d), row(of.shape[1]), row(og.shape[1]), row(om.shape[1]), _full(g.shape),
                  _full(wz.shape), _full(bz.shape), _full(wf.shape), _full(wgl.shape),
                  _full(wm.shape), _full(wo.shape)],
        out_specs=row(d),
        out_shape=jax.ShapeDtypeStruct((t, d), F32),
        compiler_params=_params(("parallel",)),
        name="merge",
    )(x2d, of, og, om, g, wz, bz, wf, wgl, wm, wo)


def _memkv_kernel(m_ref, g_ref, w_ref, k_ref, v_ref):
    h = _rms(m_ref[...], g_ref[...]).astype(BF16)
    kv = _dot(h, w_ref[...])
    w = k_ref.shape[-1]
    k_ref[...] = kv[:, :w].astype(BF16)
    v_ref[...] = kv[:, w:].astype(BF16)


def _memkv(mem2d, g, w, tm):
    t, d = mem2d.shape
    xw = w.shape[1] // 2
    return pl.pallas_call(
        _memkv_kernel,
        grid=(t // tm,),
        in_specs=[pl.BlockSpec((tm, d), lambda i: (i, 0)), _full(g.shape), _full(w.shape)],
        out_specs=[pl.BlockSpec((tm, xw), lambda i: (i, 0))] * 2,
        out_shape=[jax.ShapeDtypeStruct((t, xw), BF16)] * 2,
        compiler_params=_params(("parallel",)),
        name="memkv",
    )(mem2d, g, w)


def _xattn_kernel(x_ref, k_ref, v_ref, g_ref, wq_ref, wo_ref, o_ref):
    x = x_ref[...]
    h = _rms(x, g_ref[...]).astype(BF16)
    q = _dot(h, wq_ref[...]).astype(BF16)
    outs = []
    for hd in range(XA_HEADS):
        sl = slice(hd * XA_HD, (hd + 1) * XA_HD)
        s = _dot_nt(q[:, sl], k_ref[:, sl]) * (XA_HD ** -0.5)
        e = jnp.exp(s - jnp.max(s, axis=-1, keepdims=True))
        p = e / jnp.sum(e, axis=-1, keepdims=True)
        outs.append(_dot(p.astype(BF16), v_ref[:, sl]).astype(BF16))
    o = jnp.concatenate(outs, axis=-1)
    o_ref[...] = x + _dot(o, wo_ref[...])


def _xattn(x, k, v, g, wq, wo, tm):
    b, s, d = x.shape
    m, xw = k.shape[1], k.shape[2]
    return pl.pallas_call(
        _xattn_kernel,
        grid=(b, s // tm),
        in_specs=[pl.BlockSpec((None, tm, d), lambda bi, i: (bi, i, 0)),
                  pl.BlockSpec((None, m, xw), lambda bi, i: (bi, 0, 0)),
                  pl.BlockSpec((None, m, xw), lambda bi, i: (bi, 0, 0)),
                  _full(g.shape), _full(wq.shape), _full(wo.shape)],
        out_specs=pl.BlockSpec((None, tm, d), lambda bi, i: (bi, i, 0)),
        out_shape=jax.ShapeDtypeStruct(x.shape, F32),
        compiler_params=_params(("parallel", "parallel")),
        name="xattn",
    )(x, k, v, g, wq, wo)


def _mlp_kernel(x_ref, g_ref, w1_ref, w2_ref, gf_ref, o_ref, *, ff_tile, final_norm):
    x = x_ref[...]
    h = _rms(x, g_ref[...]).astype(BF16)
    acc = x
    for f0 in range(0, w1_ref.shape[1], ff_tile):
        a = jnp.square(jnp.maximum(_dot(h, w1_ref[:, f0:f0 + ff_tile]), 0.0))
        acc = acc + _dot(a.astype(BF16), w2_ref[f0:f0 + ff_tile, :])
    o_ref[...] = _rms(acc, gf_ref[...]) if final_norm else acc


def _mlp(x2d, g, w1, w2, gf, tm, final_norm):
    t, d = x2d.shape
    row = pl.BlockSpec((tm, d), lambda i: (i, 0))
    return pl.pallas_call(
        functools.partial(_mlp_kernel, ff_tile=min(1024, w1.shape[1]), final_norm=final_norm),
        grid=(t // tm,),
        in_specs=[row, _full(g.shape), _full(w1.shape), _full(w2.shape), _full(gf.shape)],
        out_specs=row,
        out_shape=jax.ShapeDtypeStruct((t, d), F32),
        compiler_params=_params(("parallel",)),
        name="mlp",
    )(x2d, g, w1, w2, gf)


def _slot_cols(w, heads, hd):
    k = w.shape[0]
    w = w.reshape(k, heads, hd)
    return jnp.pad(w, ((0, 0), (0, 0), (0, LANES - hd))).reshape(k, heads * LANES)


REPACK_PIECES = 2


def _repack_kernel(blk_ref, par_ref, a_ref, b_ref, o_ref, *, n_cols):
    i = pl.program_id(0)
    rows = a_ref.shape[0]
    col = blk_ref[i] * LANES + lax.broadcasted_iota(jnp.int32, (rows, 2 * LANES), 1)
    window = jnp.concatenate([a_ref[...], b_ref[...]], axis=1)
    window = jnp.where(col < n_cols, window, 0.0).astype(BF16)
    r = lax.broadcasted_iota(jnp.int32, (2 * LANES, LANES), 0)
    j = lax.broadcasted_iota(jnp.int32, (2 * LANES, LANES), 1)
    sel = None
    for p in range(REPACK_PIECES):
        off = par_ref[(i * REPACK_PIECES + p) * 3]
        width = par_ref[(i * REPACK_PIECES + p) * 3 + 1]
        dst = par_ref[(i * REPACK_PIECES + p) * 3 + 2]
        hit = (r - off == j - dst) & (j >= dst) & (j < dst + width)
        sel = hit if sel is None else sel | hit
    o_ref[...] = _dot(window, sel.astype(BF16)).astype(BF16)


def _repack(w_all, layer, blocks):
    _, k, n_cols = w_all.shape
    last_blk = (n_cols - 1) // LANES
    blk, par = [], []
    for pieces in blocks:
        b0 = min(p[0] for p in pieces) // LANES
        assert len(pieces) <= REPACK_PIECES
        pieces = list(pieces) + [(b0 * LANES, 0, 0)] * (REPACK_PIECES - len(pieces))
        for src, width, dst in pieces:
            assert 0 <= src - b0 * LANES and src - b0 * LANES + width <= 2 * LANES and dst + width <= LANES
            par += [src - b0 * LANES, width, dst]
        blk.append(b0)
    grid_spec = pltpu.PrefetchScalarGridSpec(
        num_scalar_prefetch=2,
        grid=(len(blocks),),
        in_specs=[pl.BlockSpec((None, k, LANES), lambda i, blk, par: (layer, 0, blk[i])),
                  pl.BlockSpec((None, k, LANES),
                               lambda i, blk, par: (layer, 0, jnp.minimum(blk[i] + 1, last_blk)))],
        out_specs=pl.BlockSpec((k, LANES), lambda i, blk, par: (0, i)),
    )
    return pl.pallas_call(
        functools.partial(_repack_kernel, n_cols=n_cols),
        grid_spec=grid_spec,
        out_shape=jax.ShapeDtypeStruct((k, len(blocks) * LANES), BF16),
        compiler_params=_params(("arbitrary",)),
        name="repack",
    )(jnp.asarray(np.array(blk, np.int32)), jnp.asarray(np.array(par, np.int32)), w_all, w_all)


def _inproj_weights(w_all, layer):
    d = w_all.shape[1]
    fw, kw, vw = FOX_HEADS * FOX_HD, GLA_HEADS * GLA_DK, GLA_HEADS * GLA_DV
    sizes = (fw, fw, fw, FOX_HEADS, kw, kw, vw, GLA_GATE_RANK, vw,
             MLA_Q_RANK, MLA_KV_RANK, MLA_ROPE, N_BRANCH * d)
    starts = np.concatenate([[0], np.cumsum(sizes)]).tolist()
    fq, fk, fv, ff, gq, gk, gv, glow, gr, mq, mkv, mkr, zg = starts[:-1]
    half = MLA_ROPE // 2
    slots = lambda c0, heads, hd: [[(c0 + h * hd, hd, 0)] for h in range(heads)]
    dense = lambda c0, width: [[(c0 + i, LANES, 0)] for i in range(0, width, LANES)]
    blocks = (slots(fq, FOX_HEADS, FOX_HD) + slots(fk, FOX_HEADS, FOX_HD) + slots(fv, FOX_HEADS, FOX_HD)
              + dense(gq, kw) + dense(gk, kw) + dense(gv, vw) + dense(gr, vw)
              + [[(ff, FOX_HEADS, 0)], [(glow, GLA_GATE_RANK, 0)], [(mkr, MLA_ROPE, ROPE_LANE)],
                 [(mkr + half, half, ROPE_LANE), (mkr, half, ROPE_LANE + half)]]
              + dense(mq, MLA_Q_RANK) + dense(mkv, MLA_KV_RANK))
    n, tr = False, True
    outs = [(SLOTS, BF16, n), (SLOTS, BF16, n), (SLOTS, BF16, n), (kw, BF16, n), (kw, F32, n),
            (vw, BF16, tr), (vw, F32, n), (LANES, F32, n), (LANES, F32, n), (LANES, F32, n),
            (LANES, F32, n), (MLA_Q_RANK, F32, n), (MLA_KV_RANK, F32, n)]
    return _repack(w_all, layer, blocks), outs, _repack(w_all, layer, dense(zg, N_BRANCH * d))


def _mla_weights(w_uq, w_ukv):
    half = MLA_ROPE // 2
    qk = MLA_NOPE + MLA_ROPE
    r = w_uq.shape[0]
    wq = w_uq.reshape(r, MLA_HEADS, qk)
    nope, x1, x2 = wq[..., :MLA_NOPE], wq[..., MLA_NOPE:MLA_NOPE + half], wq[..., MLA_NOPE + half:]
    pad = jnp.zeros((r, MLA_HEADS, LANES - qk), F32)
    wqa = jnp.concatenate([nope, x1, x2, pad], axis=-1).reshape(r, SLOTS)
    wqb = jnp.concatenate([jnp.zeros_like(nope), x2, x1, pad], axis=-1).reshape(r, SLOTS)
    rk = w_ukv.shape[0]
    wkv = w_ukv.reshape(rk, MLA_HEADS, MLA_NOPE + MLA_VD)
    wk = _slot_cols(wkv[..., :MLA_NOPE].reshape(rk, -1), MLA_HEADS, MLA_NOPE)
    wv = _slot_cols(wkv[..., MLA_NOPE:].reshape(rk, -1), MLA_HEADS, MLA_VD)
    return wqa.astype(BF16), wqb.astype(BF16), wk.astype(BF16), wv.astype(BF16)


def _rope_tables(seq):
    half = MLA_ROPE // 2
    inv = ROPE_BASE ** (-jnp.arange(half, dtype=F32) / half)
    ang = jnp.arange(seq).astype(F32)[:, None] * inv[None, :]
    cos, sin = jnp.cos(ang), jnp.sin(ang)
    one = jnp.ones((seq, MLA_NOPE), F32)
    zero = jnp.zeros((seq, LANES - MLA_NOPE - MLA_ROPE), F32)
    ct = jnp.concatenate([one, cos, cos, zero], axis=1)
    st = jnp.concatenate([jnp.zeros_like(one), -sin, sin, zero], axis=1)
    return ct, st


def _tile(n, pref):
    t = min(n, pref)
    assert n % t == 0, (n, t)
    return t


def kernel(x, mem, g_mix, w_in, b_fox_forget, w_gla_gate, b_gla_gate, g_gla_out, g_mla_q, w_mla_uq, g_mla_kv, w_mla_ukv, b_branch_gate, w_up_fox, w_up_gla, w_up_mla, w_out, g_xa, g_mem, w_xq, w_xkv, w_xo, g_mlp, w_mlp1, w_mlp2, g_final):
    b, s, d = x.shape
    depth = w_in.shape[0]
    t = b * s
    mlen = mem.shape[1]
    tm = _tile(t, 512)
    ts = _tile(s, 512)
    tq = _tile(s, 512)
    ct, st = _rope_tables(s)
    x2d = x.reshape(t, d)
    mem2d = mem.reshape(b * mlen, d)
    row = lambda v: v.reshape(1, -1)
    for l in range(depth):
        w1, outs, wz = _inproj_weights(w_in, l)
        fq, fk, fv, gq, gk, gv, gr, ff, glow, kr, krs, cq, ckv = _inproj(x2d, row(g_mix[l]), w1, outs, tm)
        as3 = lambda a: a.reshape(b, s, a.shape[-1])
        bf = jnp.pad(b_fox_forget[l], (0, LANES - FOX_HEADS)).reshape(1, LANES)
        fqt, fka, fvt = _fox_prep(as3(fq), as3(fk), as3(fv), as3(ff), bf, ts)
        o_fox = _flash(fqt, fka, fvt, mode="causal", scale=1.0, tq=tq, tk=2 * tq)
        wqa, wqb, wk, wv = _mla_weights(w_mla_uq[l], w_mla_ukv[l])
        mqt, mk, mvt = _mla_prep(cq, ckv, kr, krs, row(g_mla_q[l]), row(g_mla_kv[l]),
                                 wqa, wqb, wk, wv, ct, st, ts, b, s)
        o_mla = _flash(mqt, mk, mvt, mode="chunk",
                       scale=(MLA_NOPE + MLA_ROPE) ** -0.5, tq=tq, tk=2 * tq)
        wg = jnp.pad(w_gla_gate[l], ((0, LANES - GLA_GATE_RANK), (0, 0))).astype(BF16)
        o_gla = _gla(as3(gq), as3(gk), gv, as3(gr), as3(glow), wg, row(b_gla_gate[l]),
                     row(g_gla_out[l]), _tile(s, 512))
        x2d = _merge(x2d, o_fox.reshape(t, -1), o_gla.reshape(t, -1), o_mla.reshape(t, -1),
                     row(g_mix[l]), wz, row(b_branch_gate[l]), w_up_fox[l].astype(BF16),
                     w_up_gla[l].astype(BF16), w_up_mla[l].astype(BF16), w_out[l].astype(BF16), tm)
        km, vm = _memkv(mem2d, row(g_mem[l]), w_xkv[l].astype(BF16), _tile(b * mlen, 512))
        xw = km.shape[-1]
        x2d = _xattn(x2d.reshape(b, s, d), km.reshape(b, mlen, xw), vm.reshape(b, mlen, xw),
                     row(g_xa[l]), w_xq[l].astype(BF16), w_xo[l].astype(BF16),
                     _tile(s, 512)).reshape(t, d)
        x2d = _mlp(x2d, row(g_mlp[l]), w_mlp1[l].astype(BF16), w_mlp2[l].astype(BF16),
                   row(g_final), tm, final_norm=(l == depth - 1))
    return x2d.reshape(b, s, d)
```

```python
import functools

import numpy as np
import jax
import jax.numpy as jnp
from jax import lax
from jax.experimental import pallas as pl
from jax.experimental.pallas import tpu as pltpu

F32 = jnp.float32
BF16 = jnp.bfloat16

CHUNK = 64
EPS = 1e-6
FOX_HEADS, FOX_HD = 4, 64
GLA_HEADS, GLA_DK, GLA_DV, GLA_GATE_RANK, GLA_TAU = 4, 64, 128, 16, 16.0
MLA_HEADS, MLA_Q_RANK, MLA_KV_RANK, MLA_NOPE, MLA_ROPE, MLA_VD = 4, 256, 128, 64, 32, 64
ROPE_BASE = 10000.0
XA_HEADS, XA_HD = 4, 128
N_BRANCH = 3

LANES = 128
MXU_DEPTH = 256
VMEM_LIMIT = 56 * 1024 * 1024

N_HEADS = 4
SLOTS = N_HEADS * LANES
MASK_VALUE = -1e30

DEC_LANE = FOX_HD
ROPE_LANE = MLA_NOPE
DEN_ROW = 64
assert DEN_ROW == FOX_HD == MLA_VD
LOG2E = 1.4426950408889634
MLA_SCALE = (MLA_NOPE + MLA_ROPE) ** -0.5


def _params(sem):
    return pltpu.CompilerParams(dimension_semantics=sem, vmem_limit_bytes=VMEM_LIMIT)


def _dot(a, b):
    return jnp.dot(a, b, preferred_element_type=F32)


def _dot_nt(a, b):
    return lax.dot_general(a, b, (((1,), (1,)), ((), ())), preferred_element_type=F32)


def _dot_tn(a, b):
    return lax.dot_general(a, b, (((0,), (0,)), ((), ())), preferred_element_type=F32)


def _rms(x, g):
    y = x * lax.rsqrt(jnp.mean(x * x, axis=-1, keepdims=True) + EPS)
    return y * g


def _log_sigmoid(x):
    return -(jnp.maximum(-x, 0.0) + jnp.log1p(jnp.exp(-jnp.abs(x))))


N_PIECES = 3


def _split3(x):
    p1 = x.astype(BF16)
    r1 = x - p1.astype(F32)
    p2 = r1.astype(BF16)
    r2 = r1 - p2.astype(F32)
    return p1, p2, r2.astype(BF16)


def _lower_tri(n):
    r = lax.broadcasted_iota(jnp.int32, (n, n), 0)
    c = lax.broadcasted_iota(jnp.int32, (n, n), 1)
    return (r >= c).astype(BF16)


def _cumsum_rows(tri, pieces):
    return _dot(tri, pieces[0]) + _dot(tri, pieces[1]) + _dot(tri, pieces[2])


def _div_pow2(x, n):
    assert n & (n - 1) == 0, n
    return jnp.right_shift(x, n.bit_length() - 1)


def _full(shape):
    return pl.BlockSpec(shape, lambda *_: (0,) * len(shape))


INPROJ_CHUNK = 2 * MXU_DEPTH


def _inproj_kernel(x_ref, g_ref, w_ref, *out_refs, cols):
    h = _rms(x_ref[...], g_ref[...]).astype(BF16)
    n = w_ref.shape[1]
    for c0 in range(0, n, INPROJ_CHUNK):
        c1 = min(c0 + INPROJ_CHUNK, n)
        y = _dot(h, w_ref[:, c0:c1])
        for o_ref, (o0, ow, transposed) in zip(out_refs, cols):
            lo, hi = max(o0, c0), min(o0 + ow, c1)
            if lo >= hi:
                continue
            piece = y[:, lo - c0:hi - c0]
            if transposed:
                o_ref[lo - o0:hi - o0, :] = piece.T.astype(o_ref.dtype)
            else:
                o_ref[:, lo - o0:hi - o0] = piece.astype(o_ref.dtype)


def _inproj(x2d, g, w, outs, tm):
    t, d = x2d.shape
    cols, c0 = [], 0
    for cw, _, tr in outs:
        cols.append((c0, cw, tr))
        c0 += cw
    return pl.pallas_call(
        functools.partial(_inproj_kernel, cols=tuple(cols)),
        grid=(t // tm,),
        in_specs=[pl.BlockSpec((tm, d), lambda i: (i, 0)), _full((1, d)), _full(w.shape)],
        out_specs=[pl.BlockSpec((cw, tm), lambda i: (0, i)) if tr else pl.BlockSpec((tm, cw), lambda i: (i, 0))
                   for cw, _, tr in outs],
        out_shape=[jax.ShapeDtypeStruct((cw, t) if tr else (t, cw), dt) for cw, dt, tr in outs],
        compiler_params=_params(("parallel",)),
        name="inproj",
    )(x2d, g, w)


def _fox_prep_kernel(fq_ref, fk_ref, fv_ref, ff_ref, bf_ref, eq_ref, ek_ref, qo_ref, ko_ref, vo_ref,
                     carry_ref, *, ts):
    @pl.when(pl.program_id(1) == 0)
    def _():
        carry_ref[...] = jnp.zeros_like(carry_ref)

    lane = lax.broadcasted_iota(jnp.int32, (ts, LANES), 1)
    live = lane < N_PIECES * FOX_HEADS
    z = jnp.where(live, ff_ref[...] + bf_ref[...], 0.0)
    lf = jnp.where(live, _log_sigmoid(z), 0.0)
    cum3 = _dot(_lower_tri(ts), jnp.concatenate(_split3(lf), axis=1))
    cum = cum3[:, :LANES] + cum3[:, LANES:2 * LANES] + cum3[:, 2 * LANES:] + carry_ref[0:1, :]
    carry_ref[...] = jnp.broadcast_to(cum[ts - 1:ts, :], carry_ref.shape)
    p1, p2, p3 = _split3(cum * LOG2E)
    one = jnp.where(lane == N_PIECES * FOX_HEADS, 1.0, 0.0).astype(BF16)
    pieces = jnp.where(lane < FOX_HEADS, p1, jnp.where(lane < 2 * FOX_HEADS, p2,
                       jnp.where(lane < 3 * FOX_HEADS, p3, one)))
    aug_q = _dot(pieces, eq_ref[...])
    aug_k = _dot(pieces, ek_ref[...])
    for h in range(FOX_HEADS):
        sl = slice(h * LANES, (h + 1) * LANES)
        qa = fq_ref[:, sl].astype(F32) * (FOX_HD ** -0.5 * LOG2E) + aug_q[:, sl]
        ka = fk_ref[:, sl].astype(F32) + aug_k[:, sl]
        qo_ref[h] = qa.T.astype(BF16)
        ko_ref[h] = ka.astype(BF16)
        v = jnp.where(lane == DEN_ROW, 1.0, fv_ref[:, sl].astype(F32))
        vo_ref[h] = v.T.astype(BF16)


def _fox_placement():
    eq = np.zeros((LANES, SLOTS), np.float32)
    ek = np.zeros((LANES, SLOTS), np.float32)
    one_lane = N_PIECES * FOX_HEADS
    for h in range(FOX_HEADS):
        base = h * LANES + DEC_LANE
        for p in range(N_PIECES):
            eq[p * FOX_HEADS + h, base + p] = 1.0
            eq[one_lane, base + N_PIECES + p] = 1.0
            ek[one_lane, base + p] = 1.0
            ek[p * FOX_HEADS + h, base + N_PIECES + p] = -1.0
    return jnp.asarray(eq, BF16), jnp.asarray(ek, BF16)


def _head_major_specs(batch, seq, ts):
    spec_t = pl.BlockSpec((None, N_HEADS, LANES, ts), lambda bi, i: (bi, 0, 0, i))
    spec_n = pl.BlockSpec((None, N_HEADS, ts, LANES), lambda bi, i: (bi, 0, i, 0))
    shape_t = jax.ShapeDtypeStruct((batch, N_HEADS, LANES, seq), BF16)
    shape_n = jax.ShapeDtypeStruct((batch, N_HEADS, seq, LANES), BF16)
    return spec_t, spec_n, shape_t, shape_n


def _fox_prep(fq, fk, fv, ff, bf, ts):
    b, s, w = fq.shape
    spec = pl.BlockSpec((None, ts, w), lambda bi, i: (bi, i, 0))
    spec_t, spec_n, shape_t, shape_n = _head_major_specs(b, s, ts)
    eq, ek = _fox_placement()
    return pl.pallas_call(
        functools.partial(_fox_prep_kernel, ts=ts),
        grid=(b, s // ts),
        in_specs=[spec, spec, spec, pl.BlockSpec((None, ts, LANES), lambda bi, i: (bi, i, 0)),
                  _full((1, LANES)), _full(eq.shape), _full(ek.shape)],
        out_specs=[spec_t, spec_n, spec_t],
        out_shape=[shape_t, shape_n, shape_t],
        scratch_shapes=[pltpu.VMEM((8, LANES), F32)],
        compiler_params=_params(("parallel", "arbitrary")),
        name="fox_prep",
    )(fq, fk, fv, ff, bf, eq, ek)


def _mla_prep_kernel(cq_ref, ckv_ref, kr_ref, krs_ref, gq_ref, gkv_ref, wqa_ref, wqb_ref,
                     wk_ref, wv_ref, ct_ref, st_ref, q_out, k_out, v_out, *, ts):
    hq = _rms(cq_ref[...], gq_ref[...]).astype(BF16)
    hkv = _rms(ckv_ref[...], gkv_ref[...]).astype(BF16)
    ct = ct_ref[...]
    st = st_ref[...]
    lane = lax.broadcasted_iota(jnp.int32, (ts, LANES), 1)
    in_rope = (lane >= ROPE_LANE) & (lane < ROPE_LANE + MLA_ROPE)
    k_rope = jnp.where(in_rope, kr_ref[...] * ct + krs_ref[...] * st, 0.0)
    qa = _dot(hq, wqa_ref[...])
    qb = _dot(hq, wqb_ref[...])
    kk = _dot(hkv, wk_ref[...])
    vv = _dot(hkv, wv_ref[...])
    for h in range(MLA_HEADS):
        sl = slice(h * LANES, (h + 1) * LANES)
        q_out[h] = ((qa[:, sl] * ct + qb[:, sl] * st) * (MLA_SCALE * LOG2E)).T.astype(BF16)
        k_out[h] = (kk[:, sl] + k_rope).astype(BF16)
        v_out[h] = jnp.where(lane == DEN_ROW, 1.0, vv[:, sl]).T.astype(BF16)


def _mla_prep(cq, ckv, kr, krs, gq, gkv, wqa, wqb, wk, wv, ct, st, ts, batch, seq):
    nblk = seq // ts
    row = lambda w: pl.BlockSpec((ts, w), lambda bi, i: (bi * nblk + i, 0))
    tab = pl.BlockSpec((ts, LANES), lambda bi, i: (i, 0))
    full = lambda a: pl.BlockSpec(a.shape, lambda bi, i: (0,) * a.ndim)
    spec_t, spec_n, shape_t, shape_n = _head_major_specs(batch, seq, ts)
    return pl.pallas_call(
        functools.partial(_mla_prep_kernel, ts=ts),
        grid=(batch, nblk),
        in_specs=[row(MLA_Q_RANK), row(MLA_KV_RANK), row(LANES), row(LANES),
                  full(gq), full(gkv), full(wqa), full(wqb), full(wk), full(wv), tab, tab],
        out_specs=[spec_t, spec_n, spec_t],
        out_shape=[shape_t, shape_n, shape_t],
        compiler_params=_params(("parallel", "parallel")),
        name="mla_prep",
    )(cq, ckv, kr, krs, gq, gkv, wqa, wqb, wk, wv, ct, st)


def _gla_kernel(gq_ref, gk_ref, gvt_ref, gr_ref, glow_ref, wg_ref, bg_ref, go_ref, o_ref,
                state_ref, *, tc):
    kw = GLA_HEADS * GLA_DK
    vw = GLA_HEADS * GLA_DV
    assert 2 * GLA_DK == LANES and GLA_DV == LANES

    @pl.when(pl.program_id(1) == 0)
    def _():
        state_ref[...] = jnp.zeros_like(state_ref)

    first_of_pair = lax.broadcasted_iota(jnp.int32, (GLA_DV, LANES), 1) < GLA_DK
    q_first = lax.broadcasted_iota(jnp.int32, (CHUNK, LANES), 1) < GLA_DK
    gr_rows = MXU_DEPTH
    ri = lax.broadcasted_iota(jnp.int32, (gr_rows, gr_rows), 0)
    ci = lax.broadcasted_iota(jnp.int32, (gr_rows, gr_rows), 1)
    tri = ((_div_pow2(ri, CHUNK) == _div_pow2(ci, CHUNK)) & (ri >= ci)).astype(BF16)
    chunk_of_row = _div_pow2(lax.broadcasted_iota(jnp.int32, (gr_rows, kw), 0), CHUNK)
    g_out = go_ref[...]
    pairs = range(GLA_HEADS // 2)
    chunks_per_group = gr_rows // CHUNK
    ends, k_decs, qs = [], [], []
    for g0 in range(0, tc, gr_rows):
        gs = slice(g0, g0 + gr_rows)
        gate = _dot(glow_ref[gs, :].astype(BF16), wg_ref[...]) + bg_ref[...]
        cum = _cumsum_rows(tri, _split3(_log_sigmoid(gate) * (1.0 / GLA_TAU)))
        g_ends = [cum[(c + 1) * CHUNK - 1:(c + 1) * CHUNK, :] for c in range(chunks_per_group)]
        end_rows = jnp.concatenate([jnp.broadcast_to(e, (CHUNK, kw)) for e in g_ends], axis=0)
        ends += g_ends
        k_decs.append(gk_ref[gs, :] * jnp.exp(end_rows - cum))
        qs.append((gq_ref[gs, :].astype(F32) * (GLA_DK ** -0.5)).astype(BF16))
    u_pairs = []
    for n in range(tc // CHUNK):
        g, c = divmod(n, chunks_per_group)
        k_c = jnp.where(chunk_of_row == c, k_decs[g], 0.0).astype(BF16)
        u_t = _dot(gvt_ref[:, g * gr_rows:(g + 1) * gr_rows], k_c)
        u_pairs.append([jnp.where(first_of_pair,
                                  u_t[2 * p * GLA_DV:(2 * p + 1) * GLA_DV, p * LANES:(p + 1) * LANES],
                                  u_t[(2 * p + 1) * GLA_DV:(2 * p + 2) * GLA_DV, p * LANES:(p + 1) * LANES])
                        for p in pairs])
    states = [state_ref[p] for p in pairs]
    chunk_states = []
    for n in range(tc // CHUNK):
        a = jnp.exp(ends[n])
        states = [a[:, p * LANES:(p + 1) * LANES] * states[p] + u_pairs[n][p] for p in pairs]
        chunk_states.append([st.astype(BF16) for st in states])
    for p in pairs:
        state_ref[p] = states[p]
    for n in range(tc // CHUNK):
        g, c = divmod(n, chunks_per_group)
        rows = slice(n * CHUNK, (n + 1) * CHUNK)
        for p in pairs:
            q_pair = qs[g][c * CHUNK:(c + 1) * CHUNK, p * LANES:(p + 1) * LANES]
            for half in range(2):
                h = 2 * p + half
                q_h = jnp.where(q_first if half == 0 else jnp.logical_not(q_first), q_pair, 0.0)
                o = _dot_nt(q_h.astype(BF16), chunk_states[n][p])
                sl = slice(h * GLA_DV, (h + 1) * GLA_DV)
                r = gr_ref[rows, sl]
                o_ref[rows, sl] = (_rms(o, g_out) * (r * jax.nn.sigmoid(r))).astype(BF16)


def _gla(gq, gk, gvt, gr, glow, wg, bg, go, tc):
    b, s, _ = gq.shape
    nblk = s // tc
    spec = lambda w: pl.BlockSpec((None, tc, w), lambda bi, i: (bi, i, 0))
    kw, vw = GLA_HEADS * GLA_DK, GLA_HEADS * GLA_DV
    return pl.pallas_call(
        functools.partial(_gla_kernel, tc=tc),
        grid=(b, nblk),
        in_specs=[spec(kw), spec(kw), pl.BlockSpec((vw, tc), lambda bi, i: (0, bi * nblk + i)),
                  spec(vw), spec(LANES), _full(wg.shape), _full(bg.shape), _full(go.shape)],
        out_specs=spec(vw),
        out_shape=jax.ShapeDtypeStruct((b, s, vw), BF16),
        scratch_shapes=[pltpu.VMEM((GLA_HEADS // 2, GLA_DV, LANES), F32)],
        compiler_params=_params(("parallel", "arbitrary")),
        name="gla",
    )(gq, gk, gvt, gr, glow, wg, bg, go)


ROWS = MXU_DEPTH
V_ROWS = 80
assert V_ROWS > DEN_ROW and V_ROWS % 16 == 0


def _flash_kernel(qi_ref, kj_ref, qt_ref, k_ref, vt_ref, o_ref, m_ref, alpha_ref, acc_ref, s_ref,
                  *, mode, tq, tk):
    step = pl.program_id(1)
    i = qi_ref[step]
    j = kj_ref[step]
    last_j = ((i + 1) * tq - 1) // tk
    sub = tq
    assert tk == 2 * sub

    @pl.when(j == 0)
    def _():
        m_ref[...] = jnp.full_like(m_ref, MASK_VALUE)
        acc_ref[...] = jnp.zeros_like(acc_ref)

    def score_chunk(item, buf, r, m_run):
        u, h, masked = item
        k0 = u * sub + r
        s = _dot(k_ref[h, k0:k0 + ROWS, :], qt_ref[h])
        if masked:
            s_pos = j * tk + k0 + lax.broadcasted_iota(jnp.int32, (ROWS, tq), 0)
            t_pos = i * tq + lax.broadcasted_iota(jnp.int32, (ROWS, tq), 1)
            if mode == "causal":
                keep = s_pos <= t_pos
            else:
                keep = _div_pow2(s_pos, CHUNK) <= _div_pow2(t_pos, CHUNK)
            s = jnp.where(keep, s, MASK_VALUE)
        s_ref[buf, r:r + ROWS, :] = s
        return jnp.maximum(m_run, jnp.max(s, axis=0, keepdims=True))

    def value_chunk(item, buf, r, m_new):
        u, h, _ = item
        p = jnp.exp2(s_ref[buf, r:r + ROWS, :] - m_new).astype(BF16)
        return _dot(vt_ref[h, 0:V_ROWS, u * sub + r:u * sub + r + ROWS], p)

    def finish_scores(h, m_prev, m_new):
        alpha_ref[h] = jnp.exp2(m_prev - m_new)
        m_ref[h] = m_new

    def body(sub_masked):
        items = [(u, h, m) for u, m in enumerate(sub_masked) for h in range(N_HEADS)]
        chunks = range(0, sub, ROWS)
        h0 = items[0][1]
        m_prev = m_ref[h0]
        m_run = m_prev
        for r in chunks:
            m_run = score_chunk(items[0], 0, r, m_run)
        finish_scores(h0, m_prev, m_run)
        for n, item in enumerate(items):
            h = item[1]
            nxt = items[n + 1] if n + 1 < len(items) else None
            m_new = m_ref[h]
            if nxt is not None:
                m_prev = m_ref[nxt[1]]
                m_run = m_prev
            pv = None
            for r in chunks:
                if nxt is not None:
                    m_run = score_chunk(nxt, (n + 1) % 2, r, m_run)
                d = value_chunk(item, n % 2, r, m_new)
                pv = d if pv is None else pv + d
            if nxt is not None:
                finish_scores(nxt[1], m_prev, m_run)
            acc_ref[h, 0:V_ROWS, :] = alpha_ref[h] * acc_ref[h, 0:V_ROWS, :] + pv

    first = j * tk
    if mode == "causal":
        full0 = first + sub - 1 <= i * tq
        full1 = first + 2 * sub - 1 <= i * tq
    else:
        full0 = (first + sub - 1) // CHUNK <= (i * tq) // CHUNK
        full1 = (first + 2 * sub - 1) // CHUNK <= (i * tq) // CHUNK

    @pl.when(full1)
    def _():
        body((False, False))

    @pl.when(full0 & jnp.logical_not(full1))
    def _():
        body((False, True))

    @pl.when(jnp.logical_not(full0))
    def _():
        body((True,))

    @pl.when(j == last_j)
    def _():
        per = LANES // DEN_ROW
        for g in range(N_HEADS // per):
            rows = [acc_ref[h, 0:DEN_ROW, :] / acc_ref[h, DEN_ROW:DEN_ROW + 1, :]
                    for h in range(g * per, (g + 1) * per)]
            o_ref[:, g * LANES:(g + 1) * LANES] = jnp.concatenate(rows, axis=0).T.astype(o_ref.dtype)


def _flash(qt, k, vt, *, mode, tq, tk):
    b, nh, s, w = k.shape
    nq, nk = s // tq, s // tk
    assert tk == 2 * tq and tq % ROWS == 0
    steps = [(i, j) for i in range(nq) for j in range(nk) if j * tk < (i + 1) * tq]
    qi = jnp.asarray(np.array([p[0] for p in steps], np.int32))
    kj = jnp.asarray(np.array([p[1] for p in steps], np.int32))
    grid_spec = pltpu.PrefetchScalarGridSpec(
        num_scalar_prefetch=2,
        grid=(b, len(steps)),
        in_specs=[pl.BlockSpec((None, nh, w, tq), lambda bi, st, qi, kj: (bi, 0, 0, qi[st])),
                  pl.BlockSpec((None, nh, tk, w), lambda bi, st, qi, kj: (bi, 0, kj[st], 0)),
                  pl.BlockSpec((None, nh, w, tk), lambda bi, st, qi, kj: (bi, 0, 0, kj[st]))],
        out_specs=pl.BlockSpec((None, tq, nh * DEN_ROW), lambda bi, st, qi, kj: (bi, qi[st], 0)),
        scratch_shapes=[pltpu.VMEM((nh, 1, tq), F32),
                        pltpu.VMEM((nh, 1, tq), F32),
                        pltpu.VMEM((nh, w, tq), F32),
                        pltpu.VMEM((2, tq, tq), F32)],
    )
    return pl.pallas_call(
        functools.partial(_flash_kernel, mode=mode, tq=tq, tk=tk),
        grid_spec=grid_spec,
        out_shape=jax.ShapeDtypeStruct((b, s, nh * DEN_ROW), BF16),
        compiler_params=_params(("parallel", "arbitrary")),
        name="flash_" + mode,
    )(qi, kj, qt, k, vt)


def _merge_kernel(x_ref, of_ref, og_ref, om_ref, g_ref, wz_ref, bz_ref, wf_ref, wgl_ref, wm_ref,
                  wo_ref, o_ref):
    x = x_ref[...]
    d = x.shape[-1]
    h = _rms(x, g_ref[...]).astype(BF16)
    y = None
    for br, (b_ref, w_ref) in enumerate(((of_ref, wf_ref), (og_ref, wgl_ref), (om_ref, wm_ref))):
        cs = slice(br * d, (br + 1) * d)
        gate = jax.nn.sigmoid(_dot(h, wz_ref[:, cs]) + bz_ref[:, cs])
        term = gate * _dot(b_ref[...], w_ref[...])
        y = term if y is None else y + term
    o_ref[...] = x + _dot(y.astype(BF16), wo_ref[...])


def _merge(x2d, of, og, om, g, wz, bz, wf, wgl, wm, wo, tm):
    t, d = x2d.shape
    row = lambda w: pl.BlockSpec((tm, w), lambda i: (i, 0))
    return pl.pallas_call(
        _merge_kernel,
        grid=(t // tm,),
        in_specs=[row(d), row(of.shape[1]), row(og.shape[1]), row(om.shape[1]), _full(g.shape),
                  _full(wz.shape), _full(bz.shape), _full(wf.shape), _full(wgl.shape),
                  _full(wm.shape), _full(wo.shape)],
        out_specs=row(d),
        out_shape=jax.ShapeDtypeStruct((t, d), F32),
        compiler_params=_params(("parallel",)),
        name="merge",
    )(x2d, of, og, om, g, wz, bz, wf, wgl, wm, wo)


def _memkv_kernel(m_ref, g_ref, w_ref, k_ref, v_ref):
    h = _rms(m_ref[...], g_ref[...]).astype(BF16)
    kv = _dot(h, w_ref[...])
    w = k_ref.shape[-1]
    k_ref[...] = kv[:, :w].astype(BF16)
    v_ref[...] = kv[:, w:].astype(BF16)


def _memkv(mem2d, g, w, tm):
    t, d = mem2d.shape
    xw = w.shape[1] // 2
    return pl.pallas_call(
        _memkv_kernel,
        grid=(t // tm,),
        in_specs=[pl.BlockSpec((tm, d), lambda i: (i, 0)), _full(g.shape), _full(w.shape)],
        out_specs=[pl.BlockSpec((tm, xw), lambda i: (i, 0))] * 2,
        out_shape=[jax.ShapeDtypeStruct((t, xw), BF16)] * 2,
        compiler_params=_params(("parallel",)),
        name="memkv",
    )(mem2d, g, w)


def _xattn_kernel(x_ref, k_ref, v_ref, g_ref, wq_ref, wo_ref, o_ref):
    x = x_ref[...]
    h = _rms(x, g_ref[...]).astype(BF16)
    q = _dot(h, wq_ref[...]).astype(BF16)
    outs = []
    for hd in range(XA_HEADS):
        sl = slice(hd * XA_HD, (hd + 1) * XA_HD)
        s = _dot_nt(q[:, sl], k_ref[:, sl]) * (XA_HD ** -0.5)
        e = jnp.exp(s - jnp.max(s, axis=-1, keepdims=True))
        p = e / jnp.sum(e, axis=-1, keepdims=True)
        outs.append(_dot(p.astype(BF16), v_ref[:, sl]).astype(BF16))
    o = jnp.concatenate(outs, axis=-1)
    o_ref[...] = x + _dot(o, wo_ref[...])


def _xattn(x, k, v, g, wq, wo, tm):
    b, s, d = x.shape
    m, xw = k.shape[1], k.shape[2]
    return pl.pallas_call(
        _xattn_kernel,
        grid=(b, s // tm),
        in_specs=[pl.BlockSpec((None, tm, d), lambda bi, i: (bi, i, 0)),
                  pl.BlockSpec((None, m, xw), lambda bi, i: (bi, 0, 0)),
                  pl.BlockSpec((None, m, xw), lambda bi, i: (bi, 0, 0)),
                  _full(g.shape), _full(wq.shape), _full(wo.shape)],
        out_specs=pl.BlockSpec((None, tm, d), lambda bi, i: (bi, i, 0)),
        out_shape=jax.ShapeDtypeStruct(x.shape, F32),
        compiler_params=_params(("parallel", "parallel")),
        name="xattn",
    )(x, k, v, g, wq, wo)


def _mlp_kernel(x_ref, g_ref, w1_ref, w2_ref, gf_ref, o_ref, *, ff_tile, final_norm):
    x = x_ref[...]
    h = _rms(x, g_ref[...]).astype(BF16)
    acc = x
    for f0 in range(0, w1_ref.shape[1], ff_tile):
        a = jnp.square(jnp.maximum(_dot(h, w1_ref[:, f0:f0 + ff_tile]), 0.0))
        acc = acc + _dot(a.astype(BF16), w2_ref[f0:f0 + ff_tile, :])
    o_ref[...] = _rms(acc, gf_ref[...]) if final_norm else acc


def _mlp(x2d, g, w1, w2, gf, tm, final_norm):
    t, d = x2d.shape
    row = pl.BlockSpec((tm, d), lambda i: (i, 0))
    return pl.pallas_call(
        functools.partial(_mlp_kernel, ff_tile=min(1024, w1.shape[1]), final_norm=final_norm),
        grid=(t // tm,),
        in_specs=[row, _full(g.shape), _full(w1.shape), _full(w2.shape), _full(gf.shape)],
        out_specs=row,
        out_shape=jax.ShapeDtypeStruct((t, d), F32),
        compiler_params=_params(("parallel",)),
        name="mlp",
    )(x2d, g, w1, w2, gf)


def _slot_cols(w, heads, hd):
    k = w.shape[0]
    w = w.reshape(k, heads, hd)
    return jnp.pad(w, ((0, 0), (0, 0), (0, LANES - hd))).reshape(k, heads * LANES)


REPACK_PIECES = 3


def _repack_kernel(blk_ref, par_ref, a_ref, b_ref, o_ref, *, n_cols):
    i = pl.program_id(0)
    rows = a_ref.shape[0]
    col = blk_ref[i] * LANES + lax.broadcasted_iota(jnp.int32, (rows, 2 * LANES), 1)
    window = jnp.concatenate([a_ref[...], b_ref[...]], axis=1)
    window = jnp.where(col < n_cols, window, 0.0).astype(BF16)
    r = lax.broadcasted_iota(jnp.int32, (2 * LANES, LANES), 0)
    j = lax.broadcasted_iota(jnp.int32, (2 * LANES, LANES), 1)
    sel = None
    for p in range(REPACK_PIECES):
        off = par_ref[(i * REPACK_PIECES + p) * 3]
        width = par_ref[(i * REPACK_PIECES + p) * 3 + 1]
        dst = par_ref[(i * REPACK_PIECES + p) * 3 + 2]
        hit = (r - off == j - dst) & (j >= dst) & (j < dst + width)
        sel = hit if sel is None else sel | hit
    o_ref[...] = _dot(window, sel.astype(BF16)).astype(BF16)


def _repack(w_all, layer, blocks):
    _, k, n_cols = w_all.shape
    last_blk = (n_cols - 1) // LANES
    blk, par = [], []
    for pieces in blocks:
        b0 = min(p[0] for p in pieces) // LANES
        assert len(pieces) <= REPACK_PIECES
        pieces = list(pieces) + [(b0 * LANES, 0, 0)] * (REPACK_PIECES - len(pieces))
        for src, width, dst in pieces:
            assert 0 <= src - b0 * LANES and src - b0 * LANES + width <= 2 * LANES and dst + width <= LANES
            par += [src - b0 * LANES, width, dst]
        blk.append(b0)
    grid_spec = pltpu.PrefetchScalarGridSpec(
        num_scalar_prefetch=2,
        grid=(len(blocks),),
        in_specs=[pl.BlockSpec((None, k, LANES), lambda i, blk, par: (layer, 0, blk[i])),
                  pl.BlockSpec((None, k, LANES),
                               lambda i, blk, par: (layer, 0, jnp.minimum(blk[i] + 1, last_blk)))],
        out_specs=pl.BlockSpec((k, LANES), lambda i, blk, par: (0, i)),
    )
    return pl.pallas_call(
        functools.partial(_repack_kernel, n_cols=n_cols),
        grid_spec=grid_spec,
        out_shape=jax.ShapeDtypeStruct((k, len(blocks) * LANES), BF16),
        compiler_params=_params(("arbitrary",)),
        name="repack",
    )(jnp.asarray(np.array(blk, np.int32)), jnp.asarray(np.array(par, np.int32)), w_all, w_all)


def _inproj_weights(w_all, layer):
    d = w_all.shape[1]
    fw, kw, vw = FOX_HEADS * FOX_HD, GLA_HEADS * GLA_DK, GLA_HEADS * GLA_DV
    sizes = (fw, fw, fw, FOX_HEADS, kw, kw, vw, GLA_GATE_RANK, vw,
             MLA_Q_RANK, MLA_KV_RANK, MLA_ROPE, N_BRANCH * d)
    starts = np.concatenate([[0], np.cumsum(sizes)]).tolist()
    fq, fk, fv, ff, gq, gk, gv, glow, gr, mq, mkv, mkr, zg = starts[:-1]
    half = MLA_ROPE // 2
    slots = lambda c0, heads, hd: [[(c0 + h * hd, hd, 0)] for h in range(heads)]
    dense = lambda c0, width: [[(c0 + i, LANES, 0)] for i in range(0, width, LANES)]
    blocks = (slots(fq, FOX_HEADS, FOX_HD) + slots(fk, FOX_HEADS, FOX_HD) + slots(fv, FOX_HEADS, FOX_HD)
              + dense(gq, kw) + dense(gk, kw) + dense(gv, vw) + dense(gr, vw)
              + [[(ff, FOX_HEADS, p * FOX_HEADS) for p in range(N_PIECES)],
                 [(glow, GLA_GATE_RANK, 0)], [(mkr, MLA_ROPE, ROPE_LANE)],
                 [(mkr + half, half, ROPE_LANE), (mkr, half, ROPE_LANE + half)]]
              + dense(mq, MLA_Q_RANK) + dense(mkv, MLA_KV_RANK))
    n, tr = False, True
    outs = [(SLOTS, BF16, n), (SLOTS, BF16, n), (SLOTS, BF16, n), (kw, BF16, n), (kw, F32, n),
            (vw, BF16, tr), (vw, F32, n), (LANES, F32, n), (LANES, F32, n), (LANES, F32, n),
            (LANES, F32, n), (MLA_Q_RANK, F32, n), (MLA_KV_RANK, F32, n)]
    return _repack(w_all, layer, blocks), outs, _repack(w_all, layer, dense(zg, N_BRANCH * d))


def _mla_weights(w_uq, w_ukv):
    half = MLA_ROPE // 2
    qk = MLA_NOPE + MLA_ROPE
    r = w_uq.shape[0]
    wq = w_uq.reshape(r, MLA_HEADS, qk)
    nope, x1, x2 = wq[..., :MLA_NOPE], wq[..., MLA_NOPE:MLA_NOPE + half], wq[..., MLA_NOPE + half:]
    pad = jnp.zeros((r, MLA_HEADS, LANES - qk), F32)
    wqa = jnp.concatenate([nope, x1, x2, pad], axis=-1).reshape(r, SLOTS)
    wqb = jnp.concatenate([jnp.zeros_like(nope), x2, x1, pad], axis=-1).reshape(r, SLOTS)
    rk = w_ukv.shape[0]
    wkv = w_ukv.reshape(rk, MLA_HEADS, MLA_NOPE + MLA_VD)
    wk = _slot_cols(wkv[..., :MLA_NOPE].reshape(rk, -1), MLA_HEADS, MLA_NOPE)
    wv = _slot_cols(wkv[..., MLA_NOPE:].reshape(rk, -1), MLA_HEADS, MLA_VD)
    return wqa.astype(BF16), wqb.astype(BF16), wk.astype(BF16), wv.astype(BF16)


def _rope_tables(seq):
    half = MLA_ROPE // 2
    inv = ROPE_BASE ** (-jnp.arange(half, dtype=F32) / half)
    ang = jnp.arange(seq).astype(F32)[:, None] * inv[None, :]
    cos, sin = jnp.cos(ang), jnp.sin(ang)
    one = jnp.ones((seq, MLA_NOPE), F32)
    zero = jnp.zeros((seq, LANES - MLA_NOPE - MLA_ROPE), F32)
    ct = jnp.concatenate([one, cos, cos, zero], axis=1)
    st = jnp.concatenate([jnp.zeros_like(one), -sin, sin, zero], axis=1)
    return ct, st


def _tile(n, pref):
    t = min(n, pref)
    assert n % t == 0, (n, t)
    return t


def kernel(x, mem, g_mix, w_in, b_fox_forget, w_gla_gate, b_gla_gate, g_gla_out, g_mla_q, w_mla_uq, g_mla_kv, w_mla_ukv, b_branch_gate, w_up_fox, w_up_gla, w_up_mla, w_out, g_xa, g_mem, w_xq, w_xkv, w_xo, g_mlp, w_mlp1, w_mlp2, g_final):
    b, s, d = x.shape
    depth = w_in.shape[0]
    t = b * s
    mlen = mem.shape[1]
    tm = _tile(t, 512)
    ts = _tile(s, 512)
    tq = _tile(s, 512)
    ct, st = _rope_tables(s)
    x2d = x.reshape(t, d)
    mem2d = mem.reshape(b * mlen, d)
    row = lambda v: v.reshape(1, -1)
    for l in range(depth):
        w1, outs, wz = _inproj_weights(w_in, l)
        fq, fk, fv, gq, gk, gv, gr, ff, glow, kr, krs, cq, ckv = _inproj(x2d, row(g_mix[l]), w1, outs, tm)
        as3 = lambda a: a.reshape(b, s, a.shape[-1])
        bf = jnp.pad(jnp.tile(b_fox_forget[l], N_PIECES), (0, LANES - N_PIECES * FOX_HEADS)).reshape(1, LANES)
        fqt, fka, fvt = _fox_prep(as3(fq), as3(fk), as3(fv), as3(ff), bf, ts)
        o_fox = _flash(fqt, fka, fvt, mode="causal", tq=tq, tk=2 * tq)
        wqa, wqb, wk, wv = _mla_weights(w_mla_uq[l], w_mla_ukv[l])
        mqt, mk, mvt = _mla_prep(cq, ckv, kr, krs, row(g_mla_q[l]), row(g_mla_kv[l]),
                                 wqa, wqb, wk, wv, ct, st, ts, b, s)
        o_mla = _flash(mqt, mk, mvt, mode="chunk", tq=tq, tk=2 * tq)
        wg = jnp.pad(w_gla_gate[l], ((0, LANES - GLA_GATE_RANK), (0, 0))).astype(BF16)
        o_gla = _gla(as3(gq), as3(gk), gv, as3(gr), as3(glow), wg, row(b_gla_gate[l]),
                     row(g_gla_out[l]), _tile(s, 1024))
        x2d = _merge(x2d, o_fox.reshape(t, -1), o_gla.reshape(t, -1), o_mla.reshape(t, -1),
                     row(g_mix[l]), wz, row(b_branch_gate[l]), w_up_fox[l].astype(BF16),
                     w_up_gla[l].astype(BF16), w_up_mla[l].astype(BF16), w_out[l].astype(BF16), tm)
        km, vm = _memkv(mem2d, row(g_mem[l]), w_xkv[l].astype(BF16), _tile(b * mlen, 512))
        xw = km.shape[-1]
        x2d = _xattn(x2d.reshape(b, s, d), km.reshape(b, mlen, xw), vm.reshape(b, mlen, xw),
                     row(g_xa[l]), w_xq[l].astype(BF16), w_xo[l].astype(BF16),
                     _tile(s, 512)).reshape(t, d)
        x2d = _mlp(x2d, row(g_mlp[l]), w_mlp1[l].astype(BF16), w_mlp2[l].astype(BF16),
                   row(g_final), tm, final_norm=(l == depth - 1))
    return x2d.reshape(b, s, d)
```

```python
import functools

import numpy as np
import jax
import jax.numpy as jnp
from jax import lax
from jax.experimental import pallas as pl
from jax.experimental.pallas import tpu as pltpu

F32 = jnp.float32
BF16 = jnp.bfloat16

CHUNK = 64
EPS = 1e-6
FOX_HEADS, FOX_HD = 4, 64
GLA_HEADS, GLA_DK, GLA_DV, GLA_GATE_RANK, GLA_TAU = 4, 64, 128, 16, 16.0
MLA_HEADS, MLA_Q_RANK, MLA_KV_RANK, MLA_NOPE, MLA_ROPE, MLA_VD = 4, 256, 128, 64, 32, 64
ROPE_BASE = 10000.0
XA_HEADS, XA_HD = 4, 128
N_BRANCH = 3

LANES = 128
MXU_DEPTH = 256
VMEM_LIMIT = 56 * 1024 * 1024

N_HEADS = 4
SLOTS = N_HEADS * LANES
MASK_VALUE = -1e30

DEC_LANE = FOX_HD
ROPE_LANE = MLA_NOPE
DEN_ROW = 64
assert DEN_ROW == FOX_HD == MLA_VD
LOG2E = 1.4426950408889634
MLA_SCALE = (MLA_NOPE + MLA_ROPE) ** -0.5


def _params(sem):
    return pltpu.CompilerParams(dimension_semantics=sem, vmem_limit_bytes=VMEM_LIMIT)


def _dot(a, b):
    return jnp.dot(a, b, preferred_element_type=F32)


def _dot_nt(a, b):
    return lax.dot_general(a, b, (((1,), (1,)), ((), ())), preferred_element_type=F32)


def _dot_tn(a, b):
    return lax.dot_general(a, b, (((0,), (0,)), ((), ())), preferred_element_type=F32)


def _rms(x, g):
    y = x * lax.rsqrt(jnp.mean(x * x, axis=-1, keepdims=True) + EPS)
    return y * g


def _log_sigmoid(x):
    return -(jnp.maximum(-x, 0.0) + jnp.log1p(jnp.exp(-jnp.abs(x))))


N_PIECES = 3


def _split3(x):
    p1 = x.astype(BF16)
    r1 = x - p1.astype(F32)
    p2 = r1.astype(BF16)
    r2 = r1 - p2.astype(F32)
    return p1, p2, r2.astype(BF16)


def _lower_tri(n):
    r = lax.broadcasted_iota(jnp.int32, (n, n), 0)
    c = lax.broadcasted_iota(jnp.int32, (n, n), 1)
    return (r >= c).astype(BF16)


def _cumsum_rows(tri, pieces):
    return _dot(tri, pieces[0]) + _dot(tri, pieces[1]) + _dot(tri, pieces[2])


def _div_pow2(x, n):
    assert n & (n - 1) == 0, n
    return jnp.right_shift(x, n.bit_length() - 1)


def _full(shape):
    return pl.BlockSpec(shape, lambda *_: (0,) * len(shape), pipeline_mode=pl.Buffered(1))


INPROJ_CHUNK = 2 * MXU_DEPTH


def _inproj_kernel(x_ref, g_ref, w_ref, *out_refs, cols):
    h = _rms(x_ref[...], g_ref[...]).astype(BF16)
    n = w_ref.shape[1]
    for c0 in range(0, n, INPROJ_CHUNK):
        c1 = min(c0 + INPROJ_CHUNK, n)
        y = _dot(h, w_ref[:, c0:c1])
        for o_ref, (o0, ow, transposed) in zip(out_refs, cols):
            lo, hi = max(o0, c0), min(o0 + ow, c1)
            if lo >= hi:
                continue
            piece = y[:, lo - c0:hi - c0]
            if transposed:
                o_ref[lo - o0:hi - o0, :] = piece.T.astype(o_ref.dtype)
            else:
                o_ref[:, lo - o0:hi - o0] = piece.astype(o_ref.dtype)


def _inproj(x2d, g, w, outs, tm):
    t, d = x2d.shape
    cols, c0 = [], 0
    for cw, _, tr in outs:
        cols.append((c0, cw, tr))
        c0 += cw
    return pl.pallas_call(
        functools.partial(_inproj_kernel, cols=tuple(cols)),
        grid=(t // tm,),
        in_specs=[pl.BlockSpec((tm, d), lambda i: (i, 0)), _full((1, d)), _full(w.shape)],
        out_specs=[pl.BlockSpec((cw, tm), lambda i: (0, i)) if tr else pl.BlockSpec((tm, cw), lambda i: (i, 0))
                   for cw, _, tr in outs],
        out_shape=[jax.ShapeDtypeStruct((cw, t) if tr else (t, cw), dt) for cw, dt, tr in outs],
        compiler_params=_params(("parallel",)),
        name="inproj",
    )(x2d, g, w)


def _fox_prep_kernel(fq_ref, fk_ref, fv_ref, ff_ref, bf_ref, eqt_ref, ek_ref, es_ref, qo_ref, ko_ref,
                     vo_ref, carry_ref, *, ts):
    @pl.when(pl.program_id(1) == 0)
    def _():
        carry_ref[...] = jnp.zeros_like(carry_ref)

    lane = lax.broadcasted_iota(jnp.int32, (ts, LANES), 1)
    live = lane < N_PIECES * FOX_HEADS
    z = jnp.where(live, ff_ref[...] + bf_ref[...], 0.0)
    lf = jnp.where(live, _log_sigmoid(z), 0.0)
    cum3 = _dot(_lower_tri(ts), jnp.concatenate(_split3(lf), axis=1))
    cum = cum3[:, :LANES] + cum3[:, LANES:2 * LANES] + cum3[:, 2 * LANES:] + carry_ref[0:1, :]
    carry_ref[...] = jnp.broadcast_to(cum[ts - 1:ts, :], carry_ref.shape)
    p1, p2, p3 = _split3(cum * LOG2E)
    one = jnp.where(lane == N_PIECES * FOX_HEADS, 1.0, 0.0).astype(BF16)
    pieces = jnp.where(lane < FOX_HEADS, p1, jnp.where(lane < 2 * FOX_HEADS, p2,
                       jnp.where(lane < 3 * FOX_HEADS, p3, one)))
    q_t = fq_ref[...].astype(F32).T * (FOX_HD ** -0.5 * LOG2E)
    v_t = fv_ref[...].astype(F32).T
    aug_qt = _dot(eqt_ref[...], pieces.astype(F32).T.astype(BF16))
    k_slots = _dot(fk_ref[...], es_ref[...]) + _dot(pieces, ek_ref[...])
    den_rows = jnp.where(lax.broadcasted_iota(jnp.int32, (LANES - FOX_HD, ts), 0) == 0, 1.0, 0.0)
    for h in range(FOX_HEADS):
        ch = slice(h * FOX_HD, (h + 1) * FOX_HD)
        spare = slice(h * LANES + FOX_HD, (h + 1) * LANES)
        qo_ref[h] = jnp.concatenate([q_t[ch], aug_qt[spare]], axis=0).astype(BF16)
        ko_ref[h] = k_slots[:, h * LANES:(h + 1) * LANES].astype(BF16)
        vo_ref[h] = jnp.concatenate([v_t[ch], den_rows], axis=0).astype(BF16)


def _fox_placement():
    eq = np.zeros((LANES, SLOTS), np.float32)
    ek = np.zeros((LANES, SLOTS), np.float32)
    one_lane = N_PIECES * FOX_HEADS
    for h in range(FOX_HEADS):
        base = h * LANES + DEC_LANE
        for p in range(N_PIECES):
            eq[p * FOX_HEADS + h, base + p] = 1.0
            eq[one_lane, base + N_PIECES + p] = 1.0
            ek[one_lane, base + p] = 1.0
            ek[p * FOX_HEADS + h, base + N_PIECES + p] = -1.0
    es = np.zeros((FOX_HEADS * FOX_HD, SLOTS), np.float32)
    for h in range(FOX_HEADS):
        for c in range(FOX_HD):
            es[h * FOX_HD + c, h * LANES + c] = 1.0
    return jnp.asarray(eq.T, BF16), jnp.asarray(ek, BF16), jnp.asarray(es, BF16)


def _head_major_specs(batch, seq, ts):
    spec_t = pl.BlockSpec((None, None, N_HEADS, LANES, ts), lambda bi, i: (bi, i, 0, 0, 0))
    spec_n = pl.BlockSpec((None, N_HEADS, ts, LANES), lambda bi, i: (bi, 0, i, 0))
    shape_t = jax.ShapeDtypeStruct((batch, seq // ts, N_HEADS, LANES, ts), BF16)
    shape_n = jax.ShapeDtypeStruct((batch, N_HEADS, seq, LANES), BF16)
    return spec_t, spec_n, shape_t, shape_n


def _fox_prep(fq, fk, fv, ff, bf, ts):
    b, s, w = fq.shape
    spec = pl.BlockSpec((None, ts, w), lambda bi, i: (bi, i, 0))
    spec_t, spec_n, shape_t, shape_n = _head_major_specs(b, s, ts)
    eq, ek, es = _fox_placement()
    return pl.pallas_call(
        functools.partial(_fox_prep_kernel, ts=ts),
        grid=(b, s // ts),
        in_specs=[spec, spec, spec, pl.BlockSpec((None, ts, LANES), lambda bi, i: (bi, i, 0)),
                  _full((1, LANES)), _full(eq.shape), _full(ek.shape), _full(es.shape)],
        out_specs=[spec_t, spec_n, spec_t],
        out_shape=[shape_t, shape_n, shape_t],
        scratch_shapes=[pltpu.VMEM((8, LANES), F32)],
        compiler_params=_params(("parallel", "arbitrary")),
        name="fox_prep",
    )(fq, fk, fv, ff, bf, eq, ek, es)


def _mla_prep_kernel(cq_ref, ckv_ref, kr_ref, krs_ref, gq_ref, gkv_ref, wqa_ref, wqb_ref,
                     wk_ref, wv_ref, ct_ref, st_ref, q_out, k_out, v_out, *, ts):
    hq = _rms(cq_ref[...], gq_ref[...]).astype(BF16)
    hkv = _rms(ckv_ref[...], gkv_ref[...]).astype(BF16)
    ct = ct_ref[...]
    st = st_ref[...]
    lane = lax.broadcasted_iota(jnp.int32, (ts, LANES), 1)
    in_rope = (lane >= ROPE_LANE) & (lane < ROPE_LANE + MLA_ROPE)
    k_rope = jnp.where(in_rope, kr_ref[...] * ct + krs_ref[...] * st, 0.0)
    qa = _dot(hq, wqa_ref[...])
    qb = _dot(hq, wqb_ref[...])
    kk = _dot(hkv, wk_ref[...])
    vv = _dot(hkv, wv_ref[...])
    for h in range(MLA_HEADS):
        sl = slice(h * LANES, (h + 1) * LANES)
        q_out[h] = ((qa[:, sl] * ct + qb[:, sl] * st) * (MLA_SCALE * LOG2E)).T.astype(BF16)
        k_out[h] = (kk[:, sl] + k_rope).astype(BF16)
        v_out[h] = jnp.where(lane == DEN_ROW, 1.0, vv[:, sl]).T.astype(BF16)


def _mla_prep(cq, ckv, kr, krs, gq, gkv, wqa, wqb, wk, wv, ct, st, ts, batch, seq):
    nblk = seq // ts
    row = lambda w: pl.BlockSpec((ts, w), lambda bi, i: (bi * nblk + i, 0))
    tab = pl.BlockSpec((ts, LANES), lambda bi, i: (i, 0))
    full = lambda a: pl.BlockSpec(a.shape, lambda bi, i: (0,) * a.ndim)
    spec_t, spec_n, shape_t, shape_n = _head_major_specs(batch, seq, ts)
    return pl.pallas_call(
        functools.partial(_mla_prep_kernel, ts=ts),
        grid=(batch, nblk),
        in_specs=[row(MLA_Q_RANK), row(MLA_KV_RANK), row(LANES), row(LANES),
                  full(gq), full(gkv), full(wqa), full(wqb), full(wk), full(wv), tab, tab],
        out_specs=[spec_t, spec_n, spec_t],
        out_shape=[shape_t, shape_n, shape_t],
        compiler_params=_params(("parallel", "parallel")),
        name="mla_prep",
    )(cq, ckv, kr, krs, gq, gkv, wqa, wqb, wk, wv, ct, st)


def _gla_kernel(gq_ref, gk_ref, gvt_ref, gr_ref, glow_ref, wg_ref, bg_ref, go_ref, o_ref,
                state_ref, *, tc):
    kw = GLA_HEADS * GLA_DK
    vw = GLA_HEADS * GLA_DV
    assert 2 * GLA_DK == LANES and GLA_DV == LANES

    @pl.when(pl.program_id(1) == 0)
    def _():
        state_ref[...] = jnp.zeros_like(state_ref)

    first_of_pair = lax.broadcasted_iota(jnp.int32, (GLA_DV, LANES), 1) < GLA_DK
    q_first = lax.broadcasted_iota(jnp.int32, (CHUNK, LANES), 1) < GLA_DK
    gr_rows = MXU_DEPTH
    ri = lax.broadcasted_iota(jnp.int32, (gr_rows, gr_rows), 0)
    ci = lax.broadcasted_iota(jnp.int32, (gr_rows, gr_rows), 1)
    tri = ((_div_pow2(ri, CHUNK) == _div_pow2(ci, CHUNK)) & (ri >= ci)).astype(BF16)
    chunk_of_row = _div_pow2(lax.broadcasted_iota(jnp.int32, (gr_rows, kw), 0), CHUNK)
    g_out = go_ref[...]
    pairs = range(GLA_HEADS // 2)
    chunks_per_group = gr_rows // CHUNK
    ends, k_decs, qs = [], [], []
    for g0 in range(0, tc, gr_rows):
        gs = slice(g0, g0 + gr_rows)
        gate = _dot(glow_ref[gs, :].astype(BF16), wg_ref[...]) + bg_ref[...]
        cum = _cumsum_rows(tri, _split3(_log_sigmoid(gate) * (1.0 / GLA_TAU)))
        g_ends = [cum[(c + 1) * CHUNK - 1:(c + 1) * CHUNK, :] for c in range(chunks_per_group)]
        end_rows = jnp.concatenate([jnp.broadcast_to(e, (CHUNK, kw)) for e in g_ends], axis=0)
        ends += g_ends
        k_decs.append(gk_ref[gs, :] * jnp.exp(end_rows - cum))
        qs.append((gq_ref[gs, :].astype(F32) * (GLA_DK ** -0.5)).astype(BF16))
    u_pairs = []
    for n in range(tc // CHUNK):
        g, c = divmod(n, chunks_per_group)
        k_c = jnp.where(chunk_of_row == c, k_decs[g], 0.0).astype(BF16)
        u_t = _dot(gvt_ref[:, g * gr_rows:(g + 1) * gr_rows], k_c)
        u_pairs.append([jnp.where(first_of_pair,
                                  u_t[2 * p * GLA_DV:(2 * p + 1) * GLA_DV, p * LANES:(p + 1) * LANES],
                                  u_t[(2 * p + 1) * GLA_DV:(2 * p + 2) * GLA_DV, p * LANES:(p + 1) * LANES])
                        for p in pairs])
    states = [state_ref[p] for p in pairs]
    chunk_states = []
    for n in range(tc // CHUNK):
        a = jnp.exp(ends[n])
        states = [a[:, p * LANES:(p + 1) * LANES] * states[p] + u_pairs[n][p] for p in pairs]
        chunk_states.append([st.astype(BF16) for st in states])
    for p in pairs:
        state_ref[p] = states[p]
    for n in range(tc // CHUNK):
        g, c = divmod(n, chunks_per_group)
        rows = slice(n * CHUNK, (n + 1) * CHUNK)
        for p in pairs:
            q_pair = qs[g][c * CHUNK:(c + 1) * CHUNK, p * LANES:(p + 1) * LANES]
            for half in range(2):
                h = 2 * p + half
                q_h = jnp.where(q_first if half == 0 else jnp.logical_not(q_first), q_pair, 0.0)
                o = _dot_nt(q_h.astype(BF16), chunk_states[n][p])
                sl = slice(h * GLA_DV, (h + 1) * GLA_DV)
                r = gr_ref[rows, sl]
                o_ref[rows, sl] = (_rms(o, g_out) * (r * jax.nn.sigmoid(r))).astype(BF16)


def _gla(gq, gk, gvt, gr, glow, wg, bg, go, tc):
    b, s, _ = gq.shape
    nblk = s // tc
    spec = lambda w: pl.BlockSpec((None, tc, w), lambda bi, i: (bi, i, 0))
    kw, vw = GLA_HEADS * GLA_DK, GLA_HEADS * GLA_DV
    return pl.pallas_call(
        functools.partial(_gla_kernel, tc=tc),
        grid=(b, nblk),
        in_specs=[spec(kw), spec(kw), pl.BlockSpec((vw, tc), lambda bi, i: (0, bi * nblk + i)),
                  spec(vw), spec(LANES), _full(wg.shape), _full(bg.shape), _full(go.shape)],
        out_specs=spec(vw),
        out_shape=jax.ShapeDtypeStruct((b, s, vw), BF16),
        scratch_shapes=[pltpu.VMEM((GLA_HEADS // 2, GLA_DV, LANES), F32)],
        compiler_params=_params(("parallel", "arbitrary")),
        name="gla",
    )(gq, gk, gvt, gr, glow, wg, bg, go)


ROWS = MXU_DEPTH
V_ROWS = 80
assert V_ROWS > DEN_ROW and V_ROWS % 16 == 0


def _flash_kernel(qt_ref, k_ref, vt_ref, o_ref, m_ref, alpha_ref, acc_ref, s_ref, *, mode, tq):
    i = pl.program_id(1)
    sub = tq
    assert sub % CHUNK == 0
    m_ref[...] = jnp.full_like(m_ref, MASK_VALUE)
    acc_ref[...] = jnp.zeros_like(acc_ref)

    def score_chunk(base, item, buf, r, m_run):
        u, h, masked = item
        k0 = pl.multiple_of((base + u) * sub + r, ROWS)
        s = _dot(k_ref[h, pl.ds(k0, ROWS), :], qt_ref[h])
        if masked:
            s_pos = k0 + lax.broadcasted_iota(jnp.int32, (ROWS, tq), 0)
            t_pos = i * tq + lax.broadcasted_iota(jnp.int32, (ROWS, tq), 1)
            if mode == "causal":
                keep = s_pos <= t_pos
            else:
                keep = _div_pow2(s_pos, CHUNK) <= _div_pow2(t_pos, CHUNK)
            s = jnp.where(keep, s, MASK_VALUE)
        s_ref[buf, r:r + ROWS, :] = s
        return jnp.maximum(m_run, jnp.max(s, axis=0, keepdims=True))

    def value_chunk(base, item, buf, r, m_new):
        u, h, _ = item
        p = jnp.exp2(s_ref[buf, r:r + ROWS, :] - m_new).astype(BF16)
        return _dot(vt_ref[base + u, h, 0:V_ROWS, r:r + ROWS], p)

    def finish_scores(h, m_prev, m_new):
        alpha_ref[h] = jnp.exp2(m_prev - m_new)
        m_ref[h] = m_new

    def body(base, sub_masked):
        items = [(u, h, m) for u, m in enumerate(sub_masked) for h in range(N_HEADS)]
        chunks = range(0, sub, ROWS)
        h0 = items[0][1]
        m_prev = m_ref[h0]
        m_run = m_prev
        for r in chunks:
            m_run = score_chunk(base, items[0], 0, r, m_run)
        finish_scores(h0, m_prev, m_run)
        for n, item in enumerate(items):
            h = item[1]
            nxt = items[n + 1] if n + 1 < len(items) else None
            m_new = m_ref[h]
            if nxt is not None:
                m_prev = m_ref[nxt[1]]
                m_run = m_prev
            pv = None
            for r in chunks:
                if nxt is not None:
                    m_run = score_chunk(base, nxt, (n + 1) % 2, r, m_run)
                d = value_chunk(base, item, n % 2, r, m_new)
                pv = d if pv is None else pv + d
            if nxt is not None:
                finish_scores(nxt[1], m_prev, m_run)
            acc_ref[h, 0:V_ROWS, :] = alpha_ref[h] * acc_ref[h, 0:V_ROWS, :] + pv

    def full_pair(t, carry):
        body(2 * t, (False, False))
        return carry

    lax.fori_loop(0, jnp.right_shift(i, 1), full_pair, 0)

    @pl.when(jnp.bitwise_and(i, 1) == 1)
    def _():
        body(i - 1, (False,))

    body(i, (True,))

    per = LANES // DEN_ROW
    for g in range(N_HEADS // per):
        rows = [acc_ref[h, 0:DEN_ROW, :] / acc_ref[h, DEN_ROW:DEN_ROW + 1, :]
                for h in range(g * per, (g + 1) * per)]
        o_ref[:, g * LANES:(g + 1) * LANES] = jnp.concatenate(rows, axis=0).T.astype(o_ref.dtype)


def _flash(qt, k, vt, *, mode, tq):
    b, nh, s, w = k.shape
    nq = s // tq
    assert tq % ROWS == 0 and qt.shape == vt.shape == (b, nq, nh, w, tq)
    return pl.pallas_call(
        functools.partial(_flash_kernel, mode=mode, tq=tq),
        grid=(b, nq),
        in_specs=[pl.BlockSpec((None, None, nh, w, tq), lambda bi, i: (bi, i, 0, 0, 0)),
                  pl.BlockSpec((None, nh, s, w), lambda bi, i: (bi, 0, 0, 0)),
                  pl.BlockSpec((None, nq, nh, w, tq), lambda bi, i: (bi, 0, 0, 0, 0))],
        out_specs=pl.BlockSpec((None, tq, nh * DEN_ROW), lambda bi, i: (bi, i, 0)),
        out_shape=jax.ShapeDtypeStruct((b, s, nh * DEN_ROW), BF16),
        scratch_shapes=[pltpu.VMEM((nh, 1, tq), F32),
                        pltpu.VMEM((nh, 1, tq), F32),
                        pltpu.VMEM((nh, w, tq), F32),
                        pltpu.VMEM((2, tq, tq), F32)],
        compiler_params=_params(("parallel", "arbitrary")),
        name="flash_" + mode,
    )(qt, k, vt)


def _merge_kernel(x_ref, of_ref, og_ref, om_ref, g_ref, wz_ref, bz_ref, wf_ref, wgl_ref, wm_ref,
                  wo_ref, o_ref):
    x = x_ref[...]
    d = x.shape[-1]
    h = _rms(x, g_ref[...]).astype(BF16)
    y = None
    for br, (b_ref, w_ref) in enumerate(((of_ref, wf_ref), (og_ref, wgl_ref), (om_ref, wm_ref))):
        cs = slice(br * d, (br + 1) * d)
        gate = jax.nn.sigmoid(_dot(h, wz_ref[:, cs]) + bz_ref[:, cs])
        term = gate * _dot(b_ref[...], w_ref[...])
        y = term if y is None else y + term
    o_ref[...] = x + _dot(y.astype(BF16), wo_ref[...])


def _merge(x2d, of, og, om, g, wz, bz, wf, wgl, wm, wo, tm):
    t, d = x2d.shape
    row = lambda w: pl.BlockSpec((tm, w), lambda i: (i, 0))
    return pl.pallas_call(
        _merge_kernel,
        grid=(t // tm,),
        in_specs=[row(d), row(of.shape[1]), row(og.shape[1]), row(om.shape[1]), _full(g.shape),
                  _full(wz.shape), _full(bz.shape), _full(wf.shape), _full(wgl.shape),
                  _full(wm.shape), _full(wo.shape)],
        out_specs=row(d),
        out_shape=jax.ShapeDtypeStruct((t, d), F32),
        compiler_params=_params(("parallel",)),
        name="merge",
    )(x2d, of, og, om, g, wz, bz, wf, wgl, wm, wo)


def _memkv_kernel(m_ref, g_ref, w_ref, k_ref, v_ref):
    h = _rms(m_ref[...], g_ref[...]).astype(BF16)
    kv = _dot(h, w_ref[...])
    w = k_ref.shape[-1]
    k_ref[...] = kv[:, :w].astype(BF16)
    v_ref[...] = kv[:, w:].T.astype(BF16)


def _memkv(mem2d, g, w, tm):
    t, d = mem2d.shape
    xw = w.shape[1] // 2
    return pl.pallas_call(
        _memkv_kernel,
        grid=(t // tm,),
        in_specs=[pl.BlockSpec((tm, d), lambda i: (i, 0)), _full(g.shape), _full(w.shape)],
        out_specs=[pl.BlockSpec((tm, xw), lambda i: (i, 0)), pl.BlockSpec((xw, tm), lambda i: (0, i))],
        out_shape=[jax.ShapeDtypeStruct((t, xw), BF16), jax.ShapeDtypeStruct((xw, t), BF16)],
        compiler_params=_params(("parallel",)),
        name="memkv",
    )(mem2d, g, w)


def _xattn_kernel(x_ref, k_ref, vt_ref, g_ref, wq_ref, wo_ref, o_ref):
    x = x_ref[...]
    h = _rms(x, g_ref[...]).astype(BF16)
    q = _dot(h, wq_ref[...]).astype(BF16)
    outs = []
    for hd in range(XA_HEADS):
        sl = slice(hd * XA_HD, (hd + 1) * XA_HD)
        st = _dot_nt(k_ref[:, sl], q[:, sl]) * (XA_HD ** -0.5)
        e = jnp.exp(st - jnp.max(st, axis=0, keepdims=True))
        p = e / jnp.sum(e, axis=0, keepdims=True)
        outs.append(_dot(vt_ref[sl, :], p.astype(BF16)))
    o = jnp.concatenate(outs, axis=0).T.astype(BF16)
    o_ref[...] = x + _dot(o, wo_ref[...])


def _xattn(x, k, vt, g, wq, wo, tm):
    b, s, d = x.shape
    m, xw = k.shape[1], k.shape[2]
    return pl.pallas_call(
        _xattn_kernel,
        grid=(b, s // tm),
        in_specs=[pl.BlockSpec((None, tm, d), lambda bi, i: (bi, i, 0)),
                  pl.BlockSpec((None, m, xw), lambda bi, i: (bi, 0, 0)),
                  pl.BlockSpec((xw, m), lambda bi, i: (0, bi)),
                  _full(g.shape), _full(wq.shape), _full(wo.shape)],
        out_specs=pl.BlockSpec((None, tm, d), lambda bi, i: (bi, i, 0)),
        out_shape=jax.ShapeDtypeStruct(x.shape, F32),
        compiler_params=_params(("parallel", "parallel")),
        name="xattn",
    )(x, k, vt, g, wq, wo)


def _mlp_kernel(x_ref, g_ref, w1_ref, w2_ref, gf_ref, o_ref, *, ff_tile, final_norm):
    x = x_ref[...]
    h = _rms(x, g_ref[...]).astype(BF16)
    acc = x
    for f0 in range(0, w1_ref.shape[1], ff_tile):
        a = jnp.square(jnp.maximum(_dot(h, w1_ref[:, f0:f0 + ff_tile]), 0.0))
        acc = acc + _dot(a.astype(BF16), w2_ref[f0:f0 + ff_tile, :])
    o_ref[...] = _rms(acc, gf_ref[...]) if final_norm else acc


def _mlp(x2d, g, w1, w2, gf, tm, final_norm):
    t, d = x2d.shape
    row = pl.BlockSpec((tm, d), lambda i: (i, 0))
    return pl.pallas_call(
        functools.partial(_mlp_kernel, ff_tile=min(1024, w1.shape[1]), final_norm=final_norm),
        grid=(t // tm,),
        in_specs=[row, _full(g.shape), _full(w1.shape), _full(w2.shape), _full(gf.shape)],
        out_specs=row,
        out_shape=jax.ShapeDtypeStruct((t, d), F32),
        compiler_params=_params(("parallel",)),
        name="mlp",
    )(x2d, g, w1, w2, gf)


def _slot_cols(w, heads, hd):
    k = w.shape[0]
    w = w.reshape(k, heads, hd)
    return jnp.pad(w, ((0, 0), (0, 0), (0, LANES - hd))).reshape(k, heads * LANES)


REPACK_PIECES = 3


def _repack_kernel(blk_ref, par_ref, a_ref, b_ref, o_ref, *, n_cols):
    i = pl.program_id(0)
    n_layers, k = a_ref.shape[1], a_ref.shape[2]
    row = blk_ref[i] * LANES + lax.broadcasted_iota(jnp.int32, (2 * LANES, k), 0)
    j = lax.broadcasted_iota(jnp.int32, (LANES, 2 * LANES), 0)
    r = lax.broadcasted_iota(jnp.int32, (LANES, 2 * LANES), 1)
    sel = None
    for p in range(REPACK_PIECES):
        off = par_ref[(i * REPACK_PIECES + p) * 3]
        width = par_ref[(i * REPACK_PIECES + p) * 3 + 1]
        dst = par_ref[(i * REPACK_PIECES + p) * 3 + 2]
        hit = (r - off == j - dst) & (j >= dst) & (j < dst + width)
        sel = hit if sel is None else sel | hit
    sel = sel.astype(BF16)
    for l in range(n_layers):
        window = jnp.concatenate([a_ref[:, l, :], b_ref[:, l, :]], axis=0)
        window = jnp.where(row < n_cols, window, 0.0).astype(BF16)
        o_ref[l] = _dot(sel, window).T.astype(BF16)


def _repack(w_all, blocks):
    n_layers, k, n_cols = w_all.shape
    last_blk = (n_cols - 1) // LANES
    blk, par = [], []
    for pieces in blocks:
        b0 = min(p[0] for p in pieces) // LANES
        assert len(pieces) <= REPACK_PIECES
        pieces = list(pieces) + [(b0 * LANES, 0, 0)] * (REPACK_PIECES - len(pieces))
        for src, width, dst in pieces:
            assert 0 <= src - b0 * LANES and src - b0 * LANES + width <= 2 * LANES and dst + width <= LANES
            par += [src - b0 * LANES, width, dst]
        blk.append(b0)
    w_t = jnp.transpose(w_all, (2, 0, 1))
    grid_spec = pltpu.PrefetchScalarGridSpec(
        num_scalar_prefetch=2,
        grid=(len(blocks),),
        in_specs=[pl.BlockSpec((LANES, n_layers, k), lambda i, blk, par: (blk[i], 0, 0)),
                  pl.BlockSpec((LANES, n_layers, k),
                               lambda i, blk, par: (jnp.minimum(blk[i] + 1, last_blk), 0, 0))],
        out_specs=pl.BlockSpec((n_layers, k, LANES), lambda i, blk, par: (0, 0, i)),
    )
    return pl.pallas_call(
        functools.partial(_repack_kernel, n_cols=n_cols),
        grid_spec=grid_spec,
        out_shape=jax.ShapeDtypeStruct((n_layers, k, len(blocks) * LANES), BF16),
        compiler_params=_params(("arbitrary",)),
        name="repack",
    )(jnp.asarray(np.array(blk, np.int32)), jnp.asarray(np.array(par, np.int32)), w_t, w_t)


def _inproj_weights(w_all):
    d = w_all.shape[1]
    fw, kw, vw = FOX_HEADS * FOX_HD, GLA_HEADS * GLA_DK, GLA_HEADS * GLA_DV
    sizes = (fw, fw, fw, FOX_HEADS, kw, kw, vw, GLA_GATE_RANK, vw,
             MLA_Q_RANK, MLA_KV_RANK, MLA_ROPE, N_BRANCH * d)
    starts = np.concatenate([[0], np.cumsum(sizes)]).tolist()
    fq, fk, fv, ff, gq, gk, gv, glow, gr, mq, mkv, mkr, zg = starts[:-1]
    half = MLA_ROPE // 2
    dense = lambda c0, width: [[(c0 + i, LANES, 0)] for i in range(0, width, LANES)]
    blocks = (dense(fq, fw) + dense(fk, fw) + dense(fv, fw)
              + dense(gq, kw) + dense(gk, kw) + dense(gv, vw) + dense(gr, vw)
              + [[(ff, FOX_HEADS, p * FOX_HEADS) for p in range(N_PIECES)],
                 [(glow, GLA_GATE_RANK, 0)], [(mkr, MLA_ROPE, ROPE_LANE)],
                 [(mkr + half, half, ROPE_LANE), (mkr, half, ROPE_LANE + half)]]
              + dense(mq, MLA_Q_RANK) + dense(mkv, MLA_KV_RANK))
    n, tr = False, True
    outs = [(fw, BF16, n), (fw, BF16, n), (fw, BF16, n), (kw, BF16, n), (kw, F32, n),
            (vw, BF16, tr), (vw, F32, n), (LANES, F32, n), (LANES, F32, n), (LANES, F32, n),
            (LANES, F32, n), (MLA_Q_RANK, F32, n), (MLA_KV_RANK, F32, n)]
    return _repack(w_all, blocks), outs, _repack(w_all, dense(zg, N_BRANCH * d))


def _mla_weights(w_uq, w_ukv):
    half = MLA_ROPE // 2
    qk = MLA_NOPE + MLA_ROPE
    r = w_uq.shape[0]
    wq = w_uq.reshape(r, MLA_HEADS, qk)
    nope, x1, x2 = wq[..., :MLA_NOPE], wq[..., MLA_NOPE:MLA_NOPE + half], wq[..., MLA_NOPE + half:]
    pad = jnp.zeros((r, MLA_HEADS, LANES - qk), F32)
    wqa = jnp.concatenate([nope, x1, x2, pad], axis=-1).reshape(r, SLOTS)
    wqb = jnp.concatenate([jnp.zeros_like(nope), x2, x1, pad], axis=-1).reshape(r, SLOTS)
    rk = w_ukv.shape[0]
    wkv = w_ukv.reshape(rk, MLA_HEADS, MLA_NOPE + MLA_VD)
    wk = _slot_cols(wkv[..., :MLA_NOPE].reshape(rk, -1), MLA_HEADS, MLA_NOPE)
    wv = _slot_cols(wkv[..., MLA_NOPE:].reshape(rk, -1), MLA_HEADS, MLA_VD)
    return wqa.astype(BF16), wqb.astype(BF16), wk.astype(BF16), wv.astype(BF16)


def _rope_tables(seq):
    half = MLA_ROPE // 2
    inv = ROPE_BASE ** (-jnp.arange(half, dtype=F32) / half)
    ang = jnp.arange(seq).astype(F32)[:, None] * inv[None, :]
    cos, sin = jnp.cos(ang), jnp.sin(ang)
    one = jnp.ones((seq, MLA_NOPE), F32)
    zero = jnp.zeros((seq, LANES - MLA_NOPE - MLA_ROPE), F32)
    ct = jnp.concatenate([one, cos, cos, zero], axis=1)
    st = jnp.concatenate([jnp.zeros_like(one), -sin, sin, zero], axis=1)
    return ct, st


def _tile(n, pref):
    t = min(n, pref)
    assert n % t == 0, (n, t)
    return t


def kernel(x, mem, g_mix, w_in, b_fox_forget, w_gla_gate, b_gla_gate, g_gla_out, g_mla_q, w_mla_uq, g_mla_kv, w_mla_ukv, b_branch_gate, w_up_fox, w_up_gla, w_up_mla, w_out, g_xa, g_mem, w_xq, w_xkv, w_xo, g_mlp, w_mlp1, w_mlp2, g_final):
    b, s, d = x.shape
    depth = w_in.shape[0]
    t = b * s
    mlen = mem.shape[1]
    tm = _tile(t, 1024)
    ts = _tile(s, 512)
    tq = _tile(s, 512)
    ct, st = _rope_tables(s)
    x2d = x.reshape(t, d)
    mem2d = mem.reshape(b * mlen, d)
    row = lambda v: v.reshape(1, -1)
    w1_all, outs, wz_all = _inproj_weights(w_in)
    for l in range(depth):
        w1, wz = w1_all[l], wz_all[l]
        fq, fk, fv, gq, gk, gv, gr, ff, glow, kr, krs, cq, ckv = _inproj(x2d, row(g_mix[l]), w1, outs, tm)
        as3 = lambda a: a.reshape(b, s, a.shape[-1])
        bf = jnp.pad(jnp.tile(b_fox_forget[l], N_PIECES), (0, LANES - N_PIECES * FOX_HEADS)).reshape(1, LANES)
        fqt, fka, fvt = _fox_prep(as3(fq), as3(fk), as3(fv), as3(ff), bf, ts)
        o_fox = _flash(fqt, fka, fvt, mode="causal", tq=tq)
        wqa, wqb, wk, wv = _mla_weights(w_mla_uq[l], w_mla_ukv[l])
        mqt, mk, mvt = _mla_prep(cq, ckv, kr, krs, row(g_mla_q[l]), row(g_mla_kv[l]),
                                 wqa, wqb, wk, wv, ct, st, ts, b, s)
        o_mla = _flash(mqt, mk, mvt, mode="chunk", tq=tq)
        wg = jnp.pad(w_gla_gate[l], ((0, LANES - GLA_GATE_RANK), (0, 0))).astype(BF16)
        o_gla = _gla(as3(gq), as3(gk), gv, as3(gr), as3(glow), wg, row(b_gla_gate[l]),
                     row(g_gla_out[l]), _tile(s, 1024))
        x2d = _merge(x2d, o_fox.reshape(t, -1), o_gla.reshape(t, -1), o_mla.reshape(t, -1),
                     row(g_mix[l]), wz, row(b_branch_gate[l]), w_up_fox[l].astype(BF16),
                     w_up_gla[l].astype(BF16), w_up_mla[l].astype(BF16), w_out[l].astype(BF16), tm)
        km, vmt = _memkv(mem2d, row(g_mem[l]), w_xkv[l].astype(BF16), _tile(b * mlen, 512))
        x2d = _xattn(x2d.reshape(b, s, d), km.reshape(b, mlen, -1), vmt,
                     row(g_xa[l]), w_xq[l].astype(BF16), w_xo[l].astype(BF16),
                     _tile(s, 1024)).reshape(t, d)
        x2d = _mlp(x2d, row(g_mlp[l]), w_mlp1[l].astype(BF16), w_mlp2[l].astype(BF16),
                   row(g_final), tm, final_norm=(l == depth - 1))
    return x2d.reshape(b, s, d)
```

```python
import functools

import numpy as np
import jax
import jax.numpy as jnp
from jax import lax
from jax.experimental import pallas as pl
from jax.experimental.pallas import tpu as pltpu

F32 = jnp.float32
BF16 = jnp.bfloat16

CHUNK = 64
EPS = 1e-6
FOX_HEADS, FOX_HD = 4, 64
GLA_HEADS, GLA_DK, GLA_DV, GLA_GATE_RANK, GLA_TAU = 4, 64, 128, 16, 16.0
MLA_HEADS, MLA_Q_RANK, MLA_KV_RANK, MLA_NOPE, MLA_ROPE, MLA_VD = 4, 256, 128, 64, 32, 64
ROPE_BASE = 10000.0
XA_HEADS, XA_HD = 4, 128
N_BRANCH = 3

LANES = 128
MXU_DEPTH = 256
VMEM_LIMIT = 56 * 1024 * 1024

N_HEADS = 4
SLOTS = N_HEADS * LANES
MASK_VALUE = -1e30

DEC_LANE = FOX_HD
ROPE_LANE = MLA_NOPE
DEN_ROW = 64
assert DEN_ROW == FOX_HD == MLA_VD
LOG2E = 1.4426950408889634
MLA_SCALE = (MLA_NOPE + MLA_ROPE) ** -0.5


def _params(sem):
    return pltpu.CompilerParams(dimension_semantics=sem, vmem_limit_bytes=VMEM_LIMIT)


def _dot(a, b):
    return jnp.dot(a, b, preferred_element_type=F32)


def _dot_nt(a, b):
    return lax.dot_general(a, b, (((1,), (1,)), ((), ())), preferred_element_type=F32)


def _dot_tn(a, b):
    return lax.dot_general(a, b, (((0,), (0,)), ((), ())), preferred_element_type=F32)


def _rms(x, g):
    y = x * lax.rsqrt(jnp.mean(x * x, axis=-1, keepdims=True) + EPS)
    return y * g


def _log_sigmoid(x):
    return -(jnp.maximum(-x, 0.0) + jnp.log1p(jnp.exp(-jnp.abs(x))))


N_PIECES = 3


def _split3(x):
    p1 = x.astype(BF16)
    r1 = x - p1.astype(F32)
    p2 = r1.astype(BF16)
    r2 = r1 - p2.astype(F32)
    return p1, p2, r2.astype(BF16)


def _lower_tri(n):
    r = lax.broadcasted_iota(jnp.int32, (n, n), 0)
    c = lax.broadcasted_iota(jnp.int32, (n, n), 1)
    return (r >= c).astype(BF16)


def _cumsum_rows(tri, pieces):
    return _dot(tri, pieces[0]) + _dot(tri, pieces[1]) + _dot(tri, pieces[2])


def _div_pow2(x, n):
    assert n & (n - 1) == 0, n
    return jnp.right_shift(x, n.bit_length() - 1)


def _full(shape):
    return pl.BlockSpec(shape, lambda *_: (0,) * len(shape), pipeline_mode=pl.Buffered(1))


INPROJ_CHUNK = 2 * MXU_DEPTH


def _inproj_kernel(x_ref, g_ref, w_ref, *out_refs, cols):
    h = _rms(x_ref[...], g_ref[...]).astype(BF16)
    n = w_ref.shape[1]
    for c0 in range(0, n, INPROJ_CHUNK):
        c1 = min(c0 + INPROJ_CHUNK, n)
        y = _dot(h, w_ref[:, c0:c1])
        for o_ref, (o0, ow, transposed) in zip(out_refs, cols):
            lo, hi = max(o0, c0), min(o0 + ow, c1)
            if lo >= hi:
                continue
            piece = y[:, lo - c0:hi - c0]
            if transposed:
                o_ref[lo - o0:hi - o0, :] = piece.T.astype(o_ref.dtype)
            else:
                o_ref[:, lo - o0:hi - o0] = piece.astype(o_ref.dtype)


def _inproj(x2d, g, w, outs, tm):
    t, d = x2d.shape
    cols, c0 = [], 0
    for cw, _, tr in outs:
        cols.append((c0, cw, tr))
        c0 += cw
    return pl.pallas_call(
        functools.partial(_inproj_kernel, cols=tuple(cols)),
        grid=(t // tm,),
        in_specs=[pl.BlockSpec((tm, d), lambda i: (i, 0)), _full((1, d)), _full(w.shape)],
        out_specs=[pl.BlockSpec((cw, tm), lambda i: (0, i)) if tr else pl.BlockSpec((tm, cw), lambda i: (i, 0))
                   for cw, _, tr in outs],
        out_shape=[jax.ShapeDtypeStruct((cw, t) if tr else (t, cw), dt) for cw, dt, tr in outs],
        compiler_params=_params(("parallel",)),
        name="inproj",
    )(x2d, g, w)


def _fox_prep_kernel(fq_ref, fk_ref, fv_ref, ff_ref, bf_ref, eqt_ref, ek_ref, es_ref, qo_ref, ko_ref,
                     vo_ref, carry_ref, *, ts):
    @pl.when(pl.program_id(1) == 0)
    def _():
        carry_ref[...] = jnp.zeros_like(carry_ref)

    lane = lax.broadcasted_iota(jnp.int32, (ts, LANES), 1)
    live = lane < N_PIECES * FOX_HEADS
    z = jnp.where(live, ff_ref[...] + bf_ref[...], 0.0)
    lf = jnp.where(live, _log_sigmoid(z), 0.0)
    cum3 = _dot(_lower_tri(ts), jnp.concatenate(_split3(lf), axis=1))
    cum = cum3[:, :LANES] + cum3[:, LANES:2 * LANES] + cum3[:, 2 * LANES:] + carry_ref[0:1, :]
    carry_ref[...] = jnp.broadcast_to(cum[ts - 1:ts, :], carry_ref.shape)
    p1, p2, p3 = _split3(cum * LOG2E)
    one = jnp.where(lane == N_PIECES * FOX_HEADS, 1.0, 0.0).astype(BF16)
    pieces = jnp.where(lane < FOX_HEADS, p1, jnp.where(lane < 2 * FOX_HEADS, p2,
                       jnp.where(lane < 3 * FOX_HEADS, p3, one)))
    q_t = fq_ref[...].astype(F32).T * (FOX_HD ** -0.5 * LOG2E)
    v_t = fv_ref[...].astype(F32).T
    aug_qt = _dot(eqt_ref[...], pieces.astype(F32).T.astype(BF16))
    k_slots = _dot(fk_ref[...], es_ref[...]) + _dot(pieces, ek_ref[...])
    den_rows = jnp.where(lax.broadcasted_iota(jnp.int32, (LANES - FOX_HD, ts), 0) == 0, 1.0, 0.0)
    for h in range(FOX_HEADS):
        ch = slice(h * FOX_HD, (h + 1) * FOX_HD)
        spare = slice(h * LANES + FOX_HD, (h + 1) * LANES)
        qo_ref[h] = jnp.concatenate([q_t[ch], aug_qt[spare]], axis=0).astype(BF16)
        ko_ref[h] = k_slots[:, h * LANES:(h + 1) * LANES].astype(BF16)
        vo_ref[h] = jnp.concatenate([v_t[ch], den_rows], axis=0).astype(BF16)


def _fox_placement():
    eq = np.zeros((LANES, SLOTS), np.float32)
    ek = np.zeros((LANES, SLOTS), np.float32)
    one_lane = N_PIECES * FOX_HEADS
    for h in range(FOX_HEADS):
        base = h * LANES + DEC_LANE
        for p in range(N_PIECES):
            eq[p * FOX_HEADS + h, base + p] = 1.0
            eq[one_lane, base + N_PIECES + p] = 1.0
            ek[one_lane, base + p] = 1.0
            ek[p * FOX_HEADS + h, base + N_PIECES + p] = -1.0
    es = np.zeros((FOX_HEADS * FOX_HD, SLOTS), np.float32)
    for h in range(FOX_HEADS):
        for c in range(FOX_HD):
            es[h * FOX_HD + c, h * LANES + c] = 1.0
    return jnp.asarray(eq.T, BF16), jnp.asarray(ek, BF16), jnp.asarray(es, BF16)


def _head_major_specs(batch, seq, ts):
    spec_t = pl.BlockSpec((None, None, N_HEADS, LANES, ts), lambda bi, i: (bi, i, 0, 0, 0))
    spec_n = pl.BlockSpec((None, N_HEADS, ts, LANES), lambda bi, i: (bi, 0, i, 0))
    shape_t = jax.ShapeDtypeStruct((batch, seq // ts, N_HEADS, LANES, ts), BF16)
    shape_n = jax.ShapeDtypeStruct((batch, N_HEADS, seq, LANES), BF16)
    return spec_t, spec_n, shape_t, shape_n


def _fox_prep(fq, fk, fv, ff, bf, ts):
    b, s, w = fq.shape
    spec = pl.BlockSpec((None, ts, w), lambda bi, i: (bi, i, 0))
    spec_t, spec_n, shape_t, shape_n = _head_major_specs(b, s, ts)
    eq, ek, es = _fox_placement()
    return pl.pallas_call(
        functools.partial(_fox_prep_kernel, ts=ts),
        grid=(b, s // ts),
        in_specs=[spec, spec, spec, pl.BlockSpec((None, ts, LANES), lambda bi, i: (bi, i, 0)),
                  _full((1, LANES)), _full(eq.shape), _full(ek.shape), _full(es.shape)],
        out_specs=[spec_t, spec_n, spec_t],
        out_shape=[shape_t, shape_n, shape_t],
        scratch_shapes=[pltpu.VMEM((8, LANES), F32)],
        compiler_params=_params(("parallel", "arbitrary")),
        name="fox_prep",
    )(fq, fk, fv, ff, bf, eq, ek, es)


def _mla_prep_kernel(cq_ref, ckv_ref, kr_ref, krs_ref, gq_ref, gkv_ref, wqa_ref, wqb_ref,
                     wk_ref, wv_ref, ct_ref, st_ref, q_out, k_out, v_out, *, ts):
    hq = _rms(cq_ref[...], gq_ref[...]).astype(BF16)
    hkv = _rms(ckv_ref[...], gkv_ref[...]).astype(BF16)
    ct = ct_ref[...]
    st = st_ref[...]
    lane = lax.broadcasted_iota(jnp.int32, (ts, LANES), 1)
    in_rope = (lane >= ROPE_LANE) & (lane < ROPE_LANE + MLA_ROPE)
    k_rope = jnp.where(in_rope, kr_ref[...] * ct + krs_ref[...] * st, 0.0)
    qa = _dot(hq, wqa_ref[...])
    qb = _dot(hq, wqb_ref[...])
    kk = _dot(hkv, wk_ref[...])
    vv = _dot(hkv, wv_ref[...])
    for h in range(MLA_HEADS):
        sl = slice(h * LANES, (h + 1) * LANES)
        q_out[h] = ((qa[:, sl] * ct + qb[:, sl] * st) * (MLA_SCALE * LOG2E)).T.astype(BF16)
        k_out[h] = (kk[:, sl] + k_rope).astype(BF16)
        v_out[h] = jnp.where(lane == DEN_ROW, 1.0, vv[:, sl]).T.astype(BF16)


def _mla_prep(cq, ckv, kr, krs, gq, gkv, wqa, wqb, wk, wv, ct, st, ts, batch, seq):
    nblk = seq // ts
    row = lambda w: pl.BlockSpec((ts, w), lambda bi, i: (bi * nblk + i, 0))
    tab = pl.BlockSpec((ts, LANES), lambda bi, i: (i, 0))
    full = lambda a: pl.BlockSpec(a.shape, lambda bi, i: (0,) * a.ndim)
    spec_t, spec_n, shape_t, shape_n = _head_major_specs(batch, seq, ts)
    return pl.pallas_call(
        functools.partial(_mla_prep_kernel, ts=ts),
        grid=(batch, nblk),
        in_specs=[row(MLA_Q_RANK), row(MLA_KV_RANK), row(LANES), row(LANES),
                  full(gq), full(gkv), full(wqa), full(wqb), full(wk), full(wv), tab, tab],
        out_specs=[spec_t, spec_n, spec_t],
        out_shape=[shape_t, shape_n, shape_t],
        compiler_params=_params(("parallel", "parallel")),
        name="mla_prep",
    )(cq, ckv, kr, krs, gq, gkv, wqa, wqb, wk, wv, ct, st)


def _gla_kernel(gq_ref, gk_ref, gvt_ref, gr_ref, glow_ref, wg_ref, bg_ref, go_ref, o_ref,
                state_ref, *, tc):
    kw = GLA_HEADS * GLA_DK
    vw = GLA_HEADS * GLA_DV
    assert 2 * GLA_DK == LANES and GLA_DV == LANES

    @pl.when(pl.program_id(1) == 0)
    def _():
        state_ref[...] = jnp.zeros_like(state_ref)

    first_of_pair = lax.broadcasted_iota(jnp.int32, (GLA_DV, LANES), 1) < GLA_DK
    q_first = lax.broadcasted_iota(jnp.int32, (CHUNK, LANES), 1) < GLA_DK
    gr_rows = MXU_DEPTH
    ri = lax.broadcasted_iota(jnp.int32, (gr_rows, gr_rows), 0)
    ci = lax.broadcasted_iota(jnp.int32, (gr_rows, gr_rows), 1)
    tri = ((_div_pow2(ri, CHUNK) == _div_pow2(ci, CHUNK)) & (ri >= ci)).astype(BF16)
    chunk_of_row = _div_pow2(lax.broadcasted_iota(jnp.int32, (gr_rows, kw), 0), CHUNK)
    g_out = go_ref[...]
    pairs = range(GLA_HEADS // 2)
    chunks_per_group = gr_rows // CHUNK
    ends, k_decs, qs = [], [], []
    for g0 in range(0, tc, gr_rows):
        gs = slice(g0, g0 + gr_rows)
        gate = _dot(glow_ref[gs, :].astype(BF16), wg_ref[...]) + bg_ref[...]
        cum = _cumsum_rows(tri, _split3(_log_sigmoid(gate) * (1.0 / GLA_TAU)))
        g_ends = [cum[(c + 1) * CHUNK - 1:(c + 1) * CHUNK, :] for c in range(chunks_per_group)]
        end_rows = jnp.concatenate([jnp.broadcast_to(e, (CHUNK, kw)) for e in g_ends], axis=0)
        ends += g_ends
        k_decs.append(gk_ref[gs, :] * jnp.exp(end_rows - cum))
        qs.append((gq_ref[gs, :].astype(F32) * (GLA_DK ** -0.5)).astype(BF16))
    u_pairs = []
    for n in range(tc // CHUNK):
        g, c = divmod(n, chunks_per_group)
        k_c = jnp.where(chunk_of_row == c, k_decs[g], 0.0).astype(BF16)
        u_t = _dot(gvt_ref[:, g * gr_rows:(g + 1) * gr_rows], k_c)
        u_pairs.append([jnp.where(first_of_pair,
                                  u_t[2 * p * GLA_DV:(2 * p + 1) * GLA_DV, p * LANES:(p + 1) * LANES],
                                  u_t[(2 * p + 1) * GLA_DV:(2 * p + 2) * GLA_DV, p * LANES:(p + 1) * LANES])
                        for p in pairs])
    states = [state_ref[p] for p in pairs]
    chunk_states = []
    for n in range(tc // CHUNK):
        a = jnp.exp(ends[n])
        states = [a[:, p * LANES:(p + 1) * LANES] * states[p] + u_pairs[n][p] for p in pairs]
        chunk_states.append([st.astype(BF16) for st in states])
    for p in pairs:
        state_ref[p] = states[p]
    for n in range(tc // CHUNK):
        g, c = divmod(n, chunks_per_group)
        rows = slice(n * CHUNK, (n + 1) * CHUNK)
        for p in pairs:
            q_pair = qs[g][c * CHUNK:(c + 1) * CHUNK, p * LANES:(p + 1) * LANES]
            for half in range(2):
                h = 2 * p + half
                q_h = jnp.where(q_first if half == 0 else jnp.logical_not(q_first), q_pair, 0.0)
                o = _dot_nt(q_h.astype(BF16), chunk_states[n][p])
                sl = slice(h * GLA_DV, (h + 1) * GLA_DV)
                r = gr_ref[rows, sl]
                o_ref[rows, sl] = (_rms(o, g_out) * (r * jax.nn.sigmoid(r))).astype(BF16)


def _gla(gq, gk, gvt, gr, glow, wg, bg, go, tc):
    b, s, _ = gq.shape
    nblk = s // tc
    spec = lambda w: pl.BlockSpec((None, tc, w), lambda bi, i: (bi, i, 0))
    kw, vw = GLA_HEADS * GLA_DK, GLA_HEADS * GLA_DV
    return pl.pallas_call(
        functools.partial(_gla_kernel, tc=tc),
        grid=(b, nblk),
        in_specs=[spec(kw), spec(kw), pl.BlockSpec((vw, tc), lambda bi, i: (0, bi * nblk + i)),
                  spec(vw), spec(LANES), _full(wg.shape), _full(bg.shape), _full(go.shape)],
        out_specs=spec(vw),
        out_shape=jax.ShapeDtypeStruct((b, s, vw), BF16),
        scratch_shapes=[pltpu.VMEM((GLA_HEADS // 2, GLA_DV, LANES), F32)],
        compiler_params=_params(("parallel", "arbitrary")),
        name="gla",
    )(gq, gk, gvt, gr, glow, wg, bg, go)


ROWS = MXU_DEPTH
V_ROWS = 80
assert V_ROWS > DEN_ROW and V_ROWS % 16 == 0


def _flash_kernel(qt_ref, k_ref, vt_ref, o_ref, m_ref, alpha_ref, acc_ref, s_ref, *, mode, tq):
    i = pl.program_id(1)
    sub = tq
    assert sub % CHUNK == 0
    m_ref[...] = jnp.full_like(m_ref, MASK_VALUE)
    acc_ref[...] = jnp.zeros_like(acc_ref)

    def score_chunk(base, item, buf, r, m_run):
        u, h, masked = item
        k0 = pl.multiple_of((base + u) * sub + r, ROWS)
        q0 = r if masked else 0
        s = _dot(k_ref[h, pl.ds(k0, ROWS), :], qt_ref[h, :, q0:])
        if masked:
            s_pos = k0 + lax.broadcasted_iota(jnp.int32, (ROWS, tq - q0), 0)
            t_pos = i * tq + q0 + lax.broadcasted_iota(jnp.int32, (ROWS, tq - q0), 1)
            if mode == "causal":
                keep = s_pos <= t_pos
            else:
                keep = _div_pow2(s_pos, CHUNK) <= _div_pow2(t_pos, CHUNK)
            s = jnp.where(keep, s, MASK_VALUE)
        s_ref[buf, r:r + ROWS, q0:] = s
        m_new = jnp.maximum(m_run[:, q0:], jnp.max(s, axis=0, keepdims=True))
        return m_new if q0 == 0 else jnp.concatenate([m_run[:, :q0], m_new], axis=1)

    def value_chunk(base, item, buf, r, m_new):
        u, h, masked = item
        q0 = r if masked else 0
        p = jnp.exp2(s_ref[buf, r:r + ROWS, q0:] - m_new[:, q0:]).astype(BF16)
        d = _dot(vt_ref[base + u, h, 0:V_ROWS, r:r + ROWS], p)
        return d if q0 == 0 else jnp.concatenate([jnp.zeros((V_ROWS, q0), F32), d], axis=1)

    def finish_scores(h, m_prev, m_new):
        alpha_ref[h] = jnp.exp2(m_prev - m_new)
        m_ref[h] = m_new

    def body(base, sub_masked):
        items = [(u, h, m) for u, m in enumerate(sub_masked) for h in range(N_HEADS)]
        chunks = range(0, sub, ROWS)
        h0 = items[0][1]
        m_prev = m_ref[h0]
        m_run = m_prev
        for r in chunks:
            m_run = score_chunk(base, items[0], 0, r, m_run)
        finish_scores(h0, m_prev, m_run)
        for n, item in enumerate(items):
            h = item[1]
            nxt = items[n + 1] if n + 1 < len(items) else None
            m_new = m_ref[h]
            if nxt is not None:
                m_prev = m_ref[nxt[1]]
                m_run = m_prev
            pv = None
            for r in chunks:
                if nxt is not None:
                    m_run = score_chunk(base, nxt, (n + 1) % 2, r, m_run)
                d = value_chunk(base, item, n % 2, r, m_new)
                pv = d if pv is None else pv + d
            if nxt is not None:
                finish_scores(nxt[1], m_prev, m_run)
            acc_ref[h, 0:V_ROWS, :] = alpha_ref[h] * acc_ref[h, 0:V_ROWS, :] + pv

    def full_pair(t, carry):
        body(2 * t, (False, False))
        return carry

    lax.fori_loop(0, jnp.right_shift(i, 1), full_pair, 0)

    @pl.when(jnp.bitwise_and(i, 1) == 1)
    def _():
        body(i - 1, (False,))

    body(i, (True,))

    per = LANES // DEN_ROW
    for g in range(N_HEADS // per):
        rows = [acc_ref[h, 0:DEN_ROW, :] / acc_ref[h, DEN_ROW:DEN_ROW + 1, :]
                for h in range(g * per, (g + 1) * per)]
        o_ref[:, g * LANES:(g + 1) * LANES] = jnp.concatenate(rows, axis=0).T.astype(o_ref.dtype)


def _flash(qt, k, vt, *, mode, tq):
    b, nh, s, w = k.shape
    nq = s // tq
    assert tq % ROWS == 0 and qt.shape == vt.shape == (b, nq, nh, w, tq)
    return pl.pallas_call(
        functools.partial(_flash_kernel, mode=mode, tq=tq),
        grid=(b, nq),
        in_specs=[pl.BlockSpec((None, None, nh, w, tq), lambda bi, i: (bi, i, 0, 0, 0)),
                  pl.BlockSpec((None, nh, s, w), lambda bi, i: (bi, 0, 0, 0)),
                  pl.BlockSpec((None, nq, nh, w, tq), lambda bi, i: (bi, 0, 0, 0, 0))],
        out_specs=pl.BlockSpec((None, tq, nh * DEN_ROW), lambda bi, i: (bi, i, 0)),
        out_shape=jax.ShapeDtypeStruct((b, s, nh * DEN_ROW), BF16),
        scratch_shapes=[pltpu.VMEM((nh, 1, tq), F32),
                        pltpu.VMEM((nh, 1, tq), F32),
                        pltpu.VMEM((nh, w, tq), F32),
                        pltpu.VMEM((2, tq, tq), F32)],
        compiler_params=_params(("parallel", "arbitrary")),
        name="flash_" + mode,
    )(qt, k, vt)


def _merge_kernel(x_ref, of_ref, og_ref, om_ref, g_ref, wz_ref, bz_ref, wf_ref, wgl_ref, wm_ref,
                  wo_ref, o_ref):
    x = x_ref[...]
    d = x.shape[-1]
    h = _rms(x, g_ref[...]).astype(BF16)
    y = None
    for br, (b_ref, w_ref) in enumerate(((of_ref, wf_ref), (og_ref, wgl_ref), (om_ref, wm_ref))):
        cs = slice(br * d, (br + 1) * d)
        gate = jax.nn.sigmoid(_dot(h, wz_ref[:, cs]) + bz_ref[:, cs])
        term = gate * _dot(b_ref[...], w_ref[...])
        y = term if y is None else y + term
    o_ref[...] = x + _dot(y.astype(BF16), wo_ref[...])


def _merge(x2d, of, og, om, g, wz, bz, wf, wgl, wm, wo, tm):
    t, d = x2d.shape
    row = lambda w: pl.BlockSpec((tm, w), lambda i: (i, 0))
    return pl.pallas_call(
        _merge_kernel,
        grid=(t // tm,),
        in_specs=[row(d), row(of.shape[1]), row(og.shape[1]), row(om.shape[1]), _full(g.shape),
                  _full(wz.shape), _full(bz.shape), _full(wf.shape), _full(wgl.shape),
                  _full(wm.shape), _full(wo.shape)],
        out_specs=row(d),
        out_shape=jax.ShapeDtypeStruct((t, d), F32),
        compiler_params=_params(("parallel",)),
        name="merge",
    )(x2d, of, og, om, g, wz, bz, wf, wgl, wm, wo)


def _memkv_kernel(m_ref, g_ref, w_ref, k_ref, v_ref):
    h = _rms(m_ref[...], g_ref[...]).astype(BF16)
    kv = _dot(h, w_ref[...])
    w = k_ref.shape[-1]
    k_ref[...] = kv[:, :w].astype(BF16)
    v_ref[...] = kv[:, w:].T.astype(BF16)


def _memkv(mem2d, g, w, tm):
    t, d = mem2d.shape
    xw = w.shape[1] // 2
    return pl.pallas_call(
        _memkv_kernel,
        grid=(t // tm,),
        in_specs=[pl.BlockSpec((tm, d), lambda i: (i, 0)), _full(g.shape), _full(w.shape)],
        out_specs=[pl.BlockSpec((tm, xw), lambda i: (i, 0)), pl.BlockSpec((xw, tm), lambda i: (0, i))],
        out_shape=[jax.ShapeDtypeStruct((t, xw), BF16), jax.ShapeDtypeStruct((xw, t), BF16)],
        compiler_params=_params(("parallel",)),
        name="memkv",
    )(mem2d, g, w)


def _xattn_kernel(x_ref, k_ref, vt_ref, g_ref, wq_ref, wo_ref, o_ref):
    x = x_ref[...]
    h = _rms(x, g_ref[...]).astype(BF16)
    q = _dot(h, wq_ref[...]).astype(BF16)
    heads = [slice(hd * XA_HD, (hd + 1) * XA_HD) for hd in range(XA_HEADS)]
    scores = [_dot_nt(k_ref[:, sl], q[:, sl]) * (XA_HD ** -0.5) for sl in heads]
    probs = []
    for st in scores:
        e = jnp.exp(st - jnp.max(st, axis=0, keepdims=True))
        probs.append((e / jnp.sum(e, axis=0, keepdims=True)).astype(BF16))
    outs = [_dot(vt_ref[sl, :], p) for sl, p in zip(heads, probs)]
    o = jnp.concatenate(outs, axis=0).T.astype(BF16)
    o_ref[...] = x + _dot(o, wo_ref[...])


def _xattn(x, k, vt, g, wq, wo, tm):
    b, s, d = x.shape
    m, xw = k.shape[1], k.shape[2]
    return pl.pallas_call(
        _xattn_kernel,
        grid=(b, s // tm),
        in_specs=[pl.BlockSpec((None, tm, d), lambda bi, i: (bi, i, 0)),
                  pl.BlockSpec((None, m, xw), lambda bi, i: (bi, 0, 0)),
                  pl.BlockSpec((xw, m), lambda bi, i: (0, bi)),
                  _full(g.shape), _full(wq.shape), _full(wo.shape)],
        out_specs=pl.BlockSpec((None, tm, d), lambda bi, i: (bi, i, 0)),
        out_shape=jax.ShapeDtypeStruct(x.shape, F32),
        compiler_params=_params(("parallel", "parallel")),
        name="xattn",
    )(x, k, vt, g, wq, wo)


def _mlp_kernel(x_ref, g_ref, w1_ref, w2_ref, gf_ref, o_ref, *, ff_tile, final_norm):
    x = x_ref[...]
    h = _rms(x, g_ref[...]).astype(BF16)
    acc = x
    for f0 in range(0, w1_ref.shape[1], ff_tile):
        a = jnp.square(jnp.maximum(_dot(h, w1_ref[:, f0:f0 + ff_tile]), 0.0))
        acc = acc + _dot(a.astype(BF16), w2_ref[f0:f0 + ff_tile, :])
    o_ref[...] = _rms(acc, gf_ref[...]) if final_norm else acc


def _mlp(x2d, g, w1, w2, gf, tm, final_norm):
    t, d = x2d.shape
    row = pl.BlockSpec((tm, d), lambda i: (i, 0))
    return pl.pallas_call(
        functools.partial(_mlp_kernel, ff_tile=min(1024, w1.shape[1]), final_norm=final_norm),
        grid=(t // tm,),
        in_specs=[row, _full(g.shape), _full(w1.shape), _full(w2.shape), _full(gf.shape)],
        out_specs=row,
        out_shape=jax.ShapeDtypeStruct((t, d), F32),
        compiler_params=_params(("parallel",)),
        name="mlp",
    )(x2d, g, w1, w2, gf)


def _slot_cols(w, heads, hd):
    k = w.shape[0]
    w = w.reshape(k, heads, hd)
    return jnp.pad(w, ((0, 0), (0, 0), (0, LANES - hd))).reshape(k, heads * LANES)


REPACK_PIECES = 3


def _repack_kernel(blk_ref, par_ref, a_ref, b_ref, o_ref, *, n_cols):
    i = pl.program_id(0)
    n_layers, k = a_ref.shape[1], a_ref.shape[2]
    row = blk_ref[i] * LANES + lax.broadcasted_iota(jnp.int32, (2 * LANES, k), 0)
    j = lax.broadcasted_iota(jnp.int32, (LANES, 2 * LANES), 0)
    r = lax.broadcasted_iota(jnp.int32, (LANES, 2 * LANES), 1)
    sel = None
    for p in range(REPACK_PIECES):
        off = par_ref[(i * REPACK_PIECES + p) * 3]
        width = par_ref[(i * REPACK_PIECES + p) * 3 + 1]
        dst = par_ref[(i * REPACK_PIECES + p) * 3 + 2]
        hit = (r - off == j - dst) & (j >= dst) & (j < dst + width)
        sel = hit if sel is None else sel | hit
    sel = sel.astype(BF16)
    for l in range(n_layers):
        window = jnp.concatenate([a_ref[:, l, :], b_ref[:, l, :]], axis=0)
        window = jnp.where(row < n_cols, window, 0.0).astype(BF16)
        o_ref[l] = _dot(sel, window).T.astype(BF16)


def _repack(w_all, blocks):
    n_layers, k, n_cols = w_all.shape
    last_blk = (n_cols - 1) // LANES
    blk, par = [], []
    for pieces in blocks:
        b0 = min(p[0] for p in pieces) // LANES
        assert len(pieces) <= REPACK_PIECES
        pieces = list(pieces) + [(b0 * LANES, 0, 0)] * (REPACK_PIECES - len(pieces))
        for src, width, dst in pieces:
            assert 0 <= src - b0 * LANES and src - b0 * LANES + width <= 2 * LANES and dst + width <= LANES
            par += [src - b0 * LANES, width, dst]
        blk.append(b0)
    w_t = jnp.transpose(w_all, (2, 0, 1))
    grid_spec = pltpu.PrefetchScalarGridSpec(
        num_scalar_prefetch=2,
        grid=(len(blocks),),
        in_specs=[pl.BlockSpec((LANES, n_layers, k), lambda i, blk, par: (blk[i], 0, 0)),
                  pl.BlockSpec((LANES, n_layers, k),
                               lambda i, blk, par: (jnp.minimum(blk[i] + 1, last_blk), 0, 0))],
        out_specs=pl.BlockSpec((n_layers, k, LANES), lambda i, blk, par: (0, 0, i)),
    )
    return pl.pallas_call(
        functools.partial(_repack_kernel, n_cols=n_cols),
        grid_spec=grid_spec,
        out_shape=jax.ShapeDtypeStruct((n_layers, k, len(blocks) * LANES), BF16),
        compiler_params=_params(("arbitrary",)),
        name="repack",
    )(jnp.asarray(np.array(blk, np.int32)), jnp.asarray(np.array(par, np.int32)), w_t, w_t)


def _inproj_weights(w_all):
    d = w_all.shape[1]
    fw, kw, vw = FOX_HEADS * FOX_HD, GLA_HEADS * GLA_DK, GLA_HEADS * GLA_DV
    sizes = (fw, fw, fw, FOX_HEADS, kw, kw, vw, GLA_GATE_RANK, vw,
             MLA_Q_RANK, MLA_KV_RANK, MLA_ROPE, N_BRANCH * d)
    starts = np.concatenate([[0], np.cumsum(sizes)]).tolist()
    fq, fk, fv, ff, gq, gk, gv, glow, gr, mq, mkv, mkr, zg = starts[:-1]
    half = MLA_ROPE // 2
    dense = lambda c0, width: [[(c0 + i, LANES, 0)] for i in range(0, width, LANES)]
    blocks = (dense(fq, fw) + dense(fk, fw) + dense(fv, fw)
              + dense(gq, kw) + dense(gk, kw) + dense(gv, vw) + dense(gr, vw)
              + [[(ff, FOX_HEADS, p * FOX_HEADS) for p in range(N_PIECES)],
                 [(glow, GLA_GATE_RANK, 0)], [(mkr, MLA_ROPE, ROPE_LANE)],
                 [(mkr + half, half, ROPE_LANE), (mkr, half, ROPE_LANE + half)]]
              + dense(mq, MLA_Q_RANK) + dense(mkv, MLA_KV_RANK))
    n, tr = False, True
    outs = [(fw, BF16, n), (fw, BF16, n), (fw, BF16, n), (kw, BF16, n), (kw, F32, n),
            (vw, BF16, tr), (vw, F32, n), (LANES, F32, n), (LANES, F32, n), (LANES, F32, n),
            (LANES, F32, n), (MLA_Q_RANK, F32, n), (MLA_KV_RANK, F32, n)]
    return _repack(w_all, blocks), outs, _repack(w_all, dense(zg, N_BRANCH * d))


def _mla_weights(w_uq, w_ukv):
    half = MLA_ROPE // 2
    qk = MLA_NOPE + MLA_ROPE
    r = w_uq.shape[0]
    wq = w_uq.reshape(r, MLA_HEADS, qk)
    nope, x1, x2 = wq[..., :MLA_NOPE], wq[..., MLA_NOPE:MLA_NOPE + half], wq[..., MLA_NOPE + half:]
    pad = jnp.zeros((r, MLA_HEADS, LANES - qk), F32)
    wqa = jnp.concatenate([nope, x1, x2, pad], axis=-1).reshape(r, SLOTS)
    wqb = jnp.concatenate([jnp.zeros_like(nope), x2, x1, pad], axis=-1).reshape(r, SLOTS)
    rk = w_ukv.shape[0]
    wkv = w_ukv.reshape(rk, MLA_HEADS, MLA_NOPE + MLA_VD)
    wk = _slot_cols(wkv[..., :MLA_NOPE].reshape(rk, -1), MLA_HEADS, MLA_NOPE)
    wv = _slot_cols(wkv[..., MLA_NOPE:].reshape(rk, -1), MLA_HEADS, MLA_VD)
    return wqa.astype(BF16), wqb.astype(BF16), wk.astype(BF16), wv.astype(BF16)


def _rope_tables(seq):
    half = MLA_ROPE // 2
    inv = ROPE_BASE ** (-jnp.arange(half, dtype=F32) / half)
    ang = jnp.arange(seq).astype(F32)[:, None] * inv[None, :]
    cos, sin = jnp.cos(ang), jnp.sin(ang)
    one = jnp.ones((seq, MLA_NOPE), F32)
    zero = jnp.zeros((seq, LANES - MLA_NOPE - MLA_ROPE), F32)
    ct = jnp.concatenate([one, cos, cos, zero], axis=1)
    st = jnp.concatenate([jnp.zeros_like(one), -sin, sin, zero], axis=1)
    return ct, st


def _tile(n, pref):
    t = min(n, pref)
    assert n % t == 0, (n, t)
    return t


def kernel(x, mem, g_mix, w_in, b_fox_forget, w_gla_gate, b_gla_gate, g_gla_out, g_mla_q, w_mla_uq, g_mla_kv, w_mla_ukv, b_branch_gate, w_up_fox, w_up_gla, w_up_mla, w_out, g_xa, g_mem, w_xq, w_xkv, w_xo, g_mlp, w_mlp1, w_mlp2, g_final):
    b, s, d = x.shape
    depth = w_in.shape[0]
    t = b * s
    mlen = mem.shape[1]
    tm = _tile(t, 1024)
    ts = _tile(s, 512)
    tq = _tile(s, 512)
    ct, st = _rope_tables(s)
    x2d = x.reshape(t, d)
    mem2d = mem.reshape(b * mlen, d)
    row = lambda v: v.reshape(1, -1)
    w1_all, outs, wz_all = _inproj_weights(w_in)
    for l in range(depth):
        w1, wz = w1_all[l], wz_all[l]
        fq, fk, fv, gq, gk, gv, gr, ff, glow, kr, krs, cq, ckv = _inproj(x2d, row(g_mix[l]), w1, outs, tm)
        as3 = lambda a: a.reshape(b, s, a.shape[-1])
        bf = jnp.pad(jnp.tile(b_fox_forget[l], N_PIECES), (0, LANES - N_PIECES * FOX_HEADS)).reshape(1, LANES)
        fqt, fka, fvt = _fox_prep(as3(fq), as3(fk), as3(fv), as3(ff), bf, ts)
        o_fox = _flash(fqt, fka, fvt, mode="causal", tq=tq)
        wqa, wqb, wk, wv = _mla_weights(w_mla_uq[l], w_mla_ukv[l])
        mqt, mk, mvt = _mla_prep(cq, ckv, kr, krs, row(g_mla_q[l]), row(g_mla_kv[l]),
                                 wqa, wqb, wk, wv, ct, st, ts, b, s)
        o_mla = _flash(mqt, mk, mvt, mode="chunk", tq=tq)
        wg = jnp.pad(w_gla_gate[l], ((0, LANES - GLA_GATE_RANK), (0, 0))).astype(BF16)
        o_gla = _gla(as3(gq), as3(gk), gv, as3(gr), as3(glow), wg, row(b_gla_gate[l]),
                     row(g_gla_out[l]), _tile(s, 1024))
        x2d = _merge(x2d, o_fox.reshape(t, -1), o_gla.reshape(t, -1), o_mla.reshape(t, -1),
                     row(g_mix[l]), wz, row(b_branch_gate[l]), w_up_fox[l].astype(BF16),
                     w_up_gla[l].astype(BF16), w_up_mla[l].astype(BF16), w_out[l].astype(BF16), tm)
        km, vmt = _memkv(mem2d, row(g_mem[l]), w_xkv[l].astype(BF16), _tile(b * mlen, 512))
        x2d = _xattn(x2d.reshape(b, s, d), km.reshape(b, mlen, -1), vmt,
                     row(g_xa[l]), w_xq[l].astype(BF16), w_xo[l].astype(BF16),
                     _tile(s, 1024)).reshape(t, d)
        x2d = _mlp(x2d, row(g_mlp[l]), w_mlp1[l].astype(BF16), w_mlp2[l].astype(BF16),
                   row(g_final), tm, final_norm=(l == depth - 1))
    return x2d.reshape(b, s, d)
```

```python
import functools

import numpy as np
import jax
import jax.numpy as jnp
from jax import lax
from jax.experimental import pallas as pl
from jax.experimental.pallas import tpu as pltpu

F32 = jnp.float32
BF16 = jnp.bfloat16

CHUNK = 64
EPS = 1e-6
FOX_HEADS, FOX_HD = 4, 64
GLA_HEADS, GLA_DK, GLA_DV, GLA_GATE_RANK, GLA_TAU = 4, 64, 128, 16, 16.0
MLA_HEADS, MLA_Q_RANK, MLA_KV_RANK, MLA_NOPE, MLA_ROPE, MLA_VD = 4, 256, 128, 64, 32, 64
ROPE_BASE = 10000.0
XA_HEADS, XA_HD = 4, 128
N_BRANCH = 3

LANES = 128
MXU_DEPTH = 256
VMEM_LIMIT = 56 * 1024 * 1024

N_HEADS = 4
SLOTS = N_HEADS * LANES
MASK_VALUE = -1e30

DEC_LANE = FOX_HD
ROPE_LANE = MLA_NOPE
DEN_ROW = 64
assert DEN_ROW == FOX_HD == MLA_VD
LOG2E = 1.4426950408889634
MLA_SCALE = (MLA_NOPE + MLA_ROPE) ** -0.5


def _params(sem):
    return pltpu.CompilerParams(dimension_semantics=sem, vmem_limit_bytes=VMEM_LIMIT)


def _dot(a, b):
    return jnp.dot(a, b, preferred_element_type=F32)


def _dot_nt(a, b):
    return lax.dot_general(a, b, (((1,), (1,)), ((), ())), preferred_element_type=F32)


def _dot_tn(a, b):
    return lax.dot_general(a, b, (((0,), (0,)), ((), ())), preferred_element_type=F32)


def _rms(x, g):
    y = x * lax.rsqrt(jnp.mean(x * x, axis=-1, keepdims=True) + EPS)
    return y * g


def _log_sigmoid(x):
    return -(jnp.maximum(-x, 0.0) + jnp.log1p(jnp.exp(-jnp.abs(x))))


N_PIECES = 3


def _split3(x):
    p1 = x.astype(BF16)
    r1 = x - p1.astype(F32)
    p2 = r1.astype(BF16)
    r2 = r1 - p2.astype(F32)
    return p1, p2, r2.astype(BF16)


def _lower_tri(n):
    r = lax.broadcasted_iota(jnp.int32, (n, n), 0)
    c = lax.broadcasted_iota(jnp.int32, (n, n), 1)
    return (r >= c).astype(BF16)


def _cumsum_rows(tri, pieces):
    return _dot(tri, pieces[0]) + _dot(tri, pieces[1]) + _dot(tri, pieces[2])


def _div_pow2(x, n):
    assert n & (n - 1) == 0, n
    return jnp.right_shift(x, n.bit_length() - 1)


def _full(shape):
    return pl.BlockSpec(shape, lambda *_: (0,) * len(shape), pipeline_mode=pl.Buffered(1))


INPROJ_CHUNK = 2 * MXU_DEPTH


def _inproj_kernel(x_ref, g_ref, w_ref, *out_refs, cols):
    h = _rms(x_ref[...], g_ref[...]).astype(BF16)
    n = w_ref.shape[1]
    for c0 in range(0, n, INPROJ_CHUNK):
        c1 = min(c0 + INPROJ_CHUNK, n)
        y = _dot(h, w_ref[:, c0:c1])
        for o_ref, (o0, ow, transposed) in zip(out_refs, cols):
            lo, hi = max(o0, c0), min(o0 + ow, c1)
            if lo >= hi:
                continue
            piece = y[:, lo - c0:hi - c0]
            if transposed:
                o_ref[lo - o0:hi - o0, :] = piece.T.astype(o_ref.dtype)
            else:
                o_ref[:, lo - o0:hi - o0] = piece.astype(o_ref.dtype)


def _inproj(x2d, g, w, outs, tm):
    t, d = x2d.shape
    cols, c0 = [], 0
    for cw, _, tr in outs:
        cols.append((c0, cw, tr))
        c0 += cw
    return pl.pallas_call(
        functools.partial(_inproj_kernel, cols=tuple(cols)),
        grid=(t // tm,),
        in_specs=[pl.BlockSpec((tm, d), lambda i: (i, 0)), _full((1, d)), _full(w.shape)],
        out_specs=[pl.BlockSpec((cw, tm), lambda i: (0, i)) if tr else pl.BlockSpec((tm, cw), lambda i: (i, 0))
                   for cw, _, tr in outs],
        out_shape=[jax.ShapeDtypeStruct((cw, t) if tr else (t, cw), dt) for cw, dt, tr in outs],
        compiler_params=_params(("parallel",)),
        name="inproj",
    )(x2d, g, w)


def _fox_prep_kernel(fq_ref, fk_ref, fv_ref, ff_ref, bf_ref, eqt_ref, ek_ref, es_ref, qo_ref, ko_ref,
                     vo_ref, carry_ref, *, ts):
    @pl.when(pl.program_id(1) == 0)
    def _():
        carry_ref[...] = jnp.zeros_like(carry_ref)

    lane = lax.broadcasted_iota(jnp.int32, (ts, LANES), 1)
    live = lane < N_PIECES * FOX_HEADS
    z = jnp.where(live, ff_ref[...] + bf_ref[...], 0.0)
    lf = jnp.where(live, _log_sigmoid(z), 0.0)
    cum3 = _dot(_lower_tri(ts), jnp.concatenate(_split3(lf), axis=1))
    cum = cum3[:, :LANES] + cum3[:, LANES:2 * LANES] + cum3[:, 2 * LANES:] + carry_ref[0:1, :]
    carry_ref[...] = jnp.broadcast_to(cum[ts - 1:ts, :], carry_ref.shape)
    p1, p2, p3 = _split3(cum * LOG2E)
    one = jnp.where(lane == N_PIECES * FOX_HEADS, 1.0, 0.0).astype(BF16)
    pieces = jnp.where(lane < FOX_HEADS, p1, jnp.where(lane < 2 * FOX_HEADS, p2,
                       jnp.where(lane < 3 * FOX_HEADS, p3, one)))
    q_t = fq_ref[...].astype(F32).T * (FOX_HD ** -0.5 * LOG2E)
    v_t = fv_ref[...].astype(F32).T
    aug_qt = _dot(eqt_ref[...], pieces.astype(F32).T.astype(BF16))
    k_slots = _dot(fk_ref[...], es_ref[...]) + _dot(pieces, ek_ref[...])
    den_rows = jnp.where(lax.broadcasted_iota(jnp.int32, (LANES - FOX_HD, ts), 0) == 0, 1.0, 0.0)
    for h in range(FOX_HEADS):
        ch = slice(h * FOX_HD, (h + 1) * FOX_HD)
        spare = slice(h * LANES + FOX_HD, (h + 1) * LANES)
        qo_ref[h] = jnp.concatenate([q_t[ch], aug_qt[spare]], axis=0).astype(BF16)
        ko_ref[h] = k_slots[:, h * LANES:(h + 1) * LANES].astype(BF16)
        vo_ref[h] = jnp.concatenate([v_t[ch], den_rows], axis=0).astype(BF16)


def _fox_placement():
    eq = np.zeros((LANES, SLOTS), np.float32)
    ek = np.zeros((LANES, SLOTS), np.float32)
    one_lane = N_PIECES * FOX_HEADS
    for h in range(FOX_HEADS):
        base = h * LANES + DEC_LANE
        for p in range(N_PIECES):
            eq[p * FOX_HEADS + h, base + p] = 1.0
            eq[one_lane, base + N_PIECES + p] = 1.0
            ek[one_lane, base + p] = 1.0
            ek[p * FOX_HEADS + h, base + N_PIECES + p] = -1.0
    es = np.zeros((FOX_HEADS * FOX_HD, SLOTS), np.float32)
    for h in range(FOX_HEADS):
        for c in range(FOX_HD):
            es[h * FOX_HD + c, h * LANES + c] = 1.0
    return jnp.asarray(eq.T, BF16), jnp.asarray(ek, BF16), jnp.asarray(es, BF16)


def _head_major_specs(batch, seq, ts):
    spec_t = pl.BlockSpec((None, None, N_HEADS, LANES, ts), lambda bi, i: (bi, i, 0, 0, 0))
    spec_n = pl.BlockSpec((None, N_HEADS, ts, LANES), lambda bi, i: (bi, 0, i, 0))
    shape_t = jax.ShapeDtypeStruct((batch, seq // ts, N_HEADS, LANES, ts), BF16)
    shape_n = jax.ShapeDtypeStruct((batch, N_HEADS, seq, LANES), BF16)
    return spec_t, spec_n, shape_t, shape_n


N_FOX_IN, N_MLA_IN, N_PREP_OUT = 8, 12, 3


def _prep_kernel(*refs, ts):
    fox_in = refs[:N_FOX_IN]
    mla_in = refs[N_FOX_IN:N_FOX_IN + N_MLA_IN]
    outs = refs[N_FOX_IN + N_MLA_IN:N_FOX_IN + N_MLA_IN + 2 * N_PREP_OUT]
    carry_ref = refs[-1]
    _fox_prep_kernel(*fox_in, *outs[:N_PREP_OUT], carry_ref, ts=ts)
    _mla_prep_kernel(*mla_in, *outs[N_PREP_OUT:], ts=ts)


def _prep(fq, fk, fv, ff, bf, cq, ckv, kr, krs, gq, gkv, wqa, wqb, wk, wv, ct, st, ts):
    b, s, w = fq.shape
    nblk = s // ts
    spec = pl.BlockSpec((None, ts, w), lambda bi, i: (bi, i, 0))
    row = lambda width: pl.BlockSpec((ts, width), lambda bi, i: (bi * nblk + i, 0))
    tab = pl.BlockSpec((ts, LANES), lambda bi, i: (i, 0))
    spec_t, spec_n, shape_t, shape_n = _head_major_specs(b, s, ts)
    eq, ek, es = _fox_placement()
    fox_specs = [spec, spec, spec, pl.BlockSpec((None, ts, LANES), lambda bi, i: (bi, i, 0)),
                 _full((1, LANES)), _full(eq.shape), _full(ek.shape), _full(es.shape)]
    mla_specs = [row(MLA_Q_RANK), row(MLA_KV_RANK), row(LANES), row(LANES), _full(gq.shape),
                 _full(gkv.shape), _full(wqa.shape), _full(wqb.shape), _full(wk.shape),
                 _full(wv.shape), tab, tab]
    assert len(fox_specs) == N_FOX_IN and len(mla_specs) == N_MLA_IN
    outs = pl.pallas_call(
        functools.partial(_prep_kernel, ts=ts),
        grid=(b, nblk),
        in_specs=fox_specs + mla_specs,
        out_specs=[spec_t, spec_n, spec_t] * 2,
        out_shape=[shape_t, shape_n, shape_t] * 2,
        scratch_shapes=[pltpu.VMEM((8, LANES), F32)],
        compiler_params=_params(("parallel", "arbitrary")),
        name="prep",
    )(fq, fk, fv, ff, bf, eq, ek, es, cq, ckv, kr, krs, gq, gkv, wqa, wqb, wk, wv, ct, st)
    return outs[:N_PREP_OUT], outs[N_PREP_OUT:]


def _mla_prep_kernel(cq_ref, ckv_ref, kr_ref, krs_ref, gq_ref, gkv_ref, wqa_ref, wqb_ref,
                     wk_ref, wv_ref, ct_ref, st_ref, q_out, k_out, v_out, *, ts):
    hq = _rms(cq_ref[...], gq_ref[...]).astype(BF16)
    hkv = _rms(ckv_ref[...], gkv_ref[...]).astype(BF16)
    ct = ct_ref[...]
    st = st_ref[...]
    lane = lax.broadcasted_iota(jnp.int32, (ts, LANES), 1)
    in_rope = (lane >= ROPE_LANE) & (lane < ROPE_LANE + MLA_ROPE)
    k_rope = jnp.where(in_rope, kr_ref[...] * ct + krs_ref[...] * st, 0.0)
    qa = _dot(hq, wqa_ref[...])
    qb = _dot(hq, wqb_ref[...])
    kk = _dot(hkv, wk_ref[...])
    vv = _dot(hkv, wv_ref[...])
    for h in range(MLA_HEADS):
        sl = slice(h * LANES, (h + 1) * LANES)
        q_out[h] = ((qa[:, sl] * ct + qb[:, sl] * st) * (MLA_SCALE * LOG2E)).T.astype(BF16)
        k_out[h] = (kk[:, sl] + k_rope).astype(BF16)
        v_out[h] = jnp.where(lane == DEN_ROW, 1.0, vv[:, sl]).T.astype(BF16)


def _gla_kernel(gq_ref, gk_ref, gvt_ref, gr_ref, glow_ref, wg_ref, bg_ref, go_ref, o_ref,
                state_ref, *, tc):
    kw = GLA_HEADS * GLA_DK
    vw = GLA_HEADS * GLA_DV
    assert 2 * GLA_DK == LANES and GLA_DV == LANES

    @pl.when(pl.program_id(1) == 0)
    def _():
        state_ref[...] = jnp.zeros_like(state_ref)

    first_of_pair = lax.broadcasted_iota(jnp.int32, (GLA_DV, LANES), 1) < GLA_DK
    q_first = lax.broadcasted_iota(jnp.int32, (CHUNK, LANES), 1) < GLA_DK
    gr_rows = MXU_DEPTH
    ri = lax.broadcasted_iota(jnp.int32, (gr_rows, gr_rows), 0)
    ci = lax.broadcasted_iota(jnp.int32, (gr_rows, gr_rows), 1)
    tri = ((_div_pow2(ri, CHUNK) == _div_pow2(ci, CHUNK)) & (ri >= ci)).astype(BF16)
    chunk_of_row = _div_pow2(lax.broadcasted_iota(jnp.int32, (gr_rows, kw), 0), CHUNK)
    g_out = go_ref[...]
    pairs = range(GLA_HEADS // 2)
    chunks_per_group = gr_rows // CHUNK
    ends, k_decs, qs = [], [], []
    for g0 in range(0, tc, gr_rows):
        gs = slice(g0, g0 + gr_rows)
        gate = _dot(glow_ref[gs, :].astype(BF16), wg_ref[...]) + bg_ref[...]
        cum = _cumsum_rows(tri, _split3(_log_sigmoid(gate) * (1.0 / GLA_TAU)))
        g_ends = [cum[(c + 1) * CHUNK - 1:(c + 1) * CHUNK, :] for c in range(chunks_per_group)]
        end_rows = jnp.concatenate([jnp.broadcast_to(e, (CHUNK, kw)) for e in g_ends], axis=0)
        ends += g_ends
        k_decs.append(gk_ref[gs, :] * jnp.exp(end_rows - cum))
        qs.append((gq_ref[gs, :].astype(F32) * (GLA_DK ** -0.5)).astype(BF16))
    u_pairs = []
    for n in range(tc // CHUNK):
        g, c = divmod(n, chunks_per_group)
        k_c = jnp.where(chunk_of_row == c, k_decs[g], 0.0).astype(BF16)
        u_t = _dot(gvt_ref[:, g * gr_rows:(g + 1) * gr_rows], k_c)
        u_pairs.append([jnp.where(first_of_pair,
                                  u_t[2 * p * GLA_DV:(2 * p + 1) * GLA_DV, p * LANES:(p + 1) * LANES],
                                  u_t[(2 * p + 1) * GLA_DV:(2 * p + 2) * GLA_DV, p * LANES:(p + 1) * LANES])
                        for p in pairs])
    states = [state_ref[p] for p in pairs]
    chunk_states = []
    for n in range(tc // CHUNK):
        a = jnp.exp(ends[n])
        states = [a[:, p * LANES:(p + 1) * LANES] * states[p] + u_pairs[n][p] for p in pairs]
        chunk_states.append([st.astype(BF16) for st in states])
    for p in pairs:
        state_ref[p] = states[p]
    for n in range(tc // CHUNK):
        g, c = divmod(n, chunks_per_group)
        rows = slice(n * CHUNK, (n + 1) * CHUNK)
        for p in pairs:
            q_pair = qs[g][c * CHUNK:(c + 1) * CHUNK, p * LANES:(p + 1) * LANES]
            for half in range(2):
                h = 2 * p + half
                q_h = jnp.where(q_first if half == 0 else jnp.logical_not(q_first), q_pair, 0.0)
                o = _dot_nt(q_h.astype(BF16), chunk_states[n][p])
                sl = slice(h * GLA_DV, (h + 1) * GLA_DV)
                r = gr_ref[rows, sl]
                o_ref[rows, sl] = (_rms(o, g_out) * (r * jax.nn.sigmoid(r))).astype(BF16)


def _gla(gq, gk, gvt, gr, glow, wg, bg, go, tc):
    b, s, _ = gq.shape
    nblk = s // tc
    spec = lambda w: pl.BlockSpec((None, tc, w), lambda bi, i: (bi, i, 0))
    kw, vw = GLA_HEADS * GLA_DK, GLA_HEADS * GLA_DV
    return pl.pallas_call(
        functools.partial(_gla_kernel, tc=tc),
        grid=(b, nblk),
        in_specs=[spec(kw), spec(kw), pl.BlockSpec((vw, tc), lambda bi, i: (0, bi * nblk + i)),
                  spec(vw), spec(LANES), _full(wg.shape), _full(bg.shape), _full(go.shape)],
        out_specs=spec(vw),
        out_shape=jax.ShapeDtypeStruct((b, s, vw), BF16),
        scratch_shapes=[pltpu.VMEM((GLA_HEADS // 2, GLA_DV, LANES), F32)],
        compiler_params=_params(("parallel", "arbitrary")),
        name="gla",
    )(gq, gk, gvt, gr, glow, wg, bg, go)


ROWS = MXU_DEPTH
V_ROWS = 80
assert V_ROWS > DEN_ROW and V_ROWS % 16 == 0


def _flash_kernel(qt_ref, k_ref, vt_ref, o_ref, m_ref, alpha_ref, acc_ref, s_ref, *, mode, tq):
    i = pl.program_id(1)
    sub = tq
    assert sub % CHUNK == 0
    m_ref[...] = jnp.full_like(m_ref, MASK_VALUE)
    acc_ref[...] = jnp.zeros_like(acc_ref)

    def score_chunk(base, item, buf, r, m_run):
        u, h, masked = item
        k0 = pl.multiple_of((base + u) * sub + r, ROWS)
        q0 = r if masked else 0
        s = _dot(k_ref[h, pl.ds(k0, ROWS), :], qt_ref[h, :, q0:])
        if masked:
            s_pos = k0 + lax.broadcasted_iota(jnp.int32, (ROWS, tq - q0), 0)
            t_pos = i * tq + q0 + lax.broadcasted_iota(jnp.int32, (ROWS, tq - q0), 1)
            if mode == "causal":
                keep = s_pos <= t_pos
            else:
                keep = _div_pow2(s_pos, CHUNK) <= _div_pow2(t_pos, CHUNK)
            s = jnp.where(keep, s, MASK_VALUE)
        s_ref[buf, r:r + ROWS, q0:] = s
        m_new = jnp.maximum(m_run[:, q0:], jnp.max(s, axis=0, keepdims=True))
        return m_new if q0 == 0 else jnp.concatenate([m_run[:, :q0], m_new], axis=1)

    def value_chunk(base, item, buf, r, m_new):
        u, h, masked = item
        q0 = r if masked else 0
        p = jnp.exp2(s_ref[buf, r:r + ROWS, q0:] - m_new[:, q0:]).astype(BF16)
        d = _dot(vt_ref[base + u, h, 0:V_ROWS, r:r + ROWS], p)
        return d if q0 == 0 else jnp.concatenate([jnp.zeros((V_ROWS, q0), F32), d], axis=1)

    def finish_scores(h, m_prev, m_new):
        alpha_ref[h] = jnp.exp2(m_prev - m_new)
        m_ref[h] = m_new

    def body(base, sub_masked):
        items = [(u, h, m) for u, m in enumerate(sub_masked) for h in range(N_HEADS)]
        chunks = range(0, sub, ROWS)
        h0 = items[0][1]
        m_prev = m_ref[h0]
        m_run = m_prev
        for r in chunks:
            m_run = score_chunk(base, items[0], 0, r, m_run)
        finish_scores(h0, m_prev, m_run)
        for n, item in enumerate(items):
            h = item[1]
            nxt = items[n + 1] if n + 1 < len(items) else None
            m_new = m_ref[h]
            if nxt is not None:
                m_prev = m_ref[nxt[1]]
                m_run = m_prev
            pv = None
            for r in chunks:
                if nxt is not None:
                    m_run = score_chunk(base, nxt, (n + 1) % 2, r, m_run)
                d = value_chunk(base, item, n % 2, r, m_new)
                pv = d if pv is None else pv + d
            if nxt is not None:
                finish_scores(nxt[1], m_prev, m_run)
            acc_ref[h, 0:V_ROWS, :] = alpha_ref[h] * acc_ref[h, 0:V_ROWS, :] + pv

    def full_pair(t, carry):
        body(2 * t, (False, False))
        return carry

    lax.fori_loop(0, jnp.right_shift(i, 1), full_pair, 0)

    @pl.when(jnp.bitwise_and(i, 1) == 1)
    def _():
        body(i - 1, (False, True))

    @pl.when(jnp.bitwise_and(i, 1) == 0)
    def _():
        body(i, (True,))

    per = LANES // DEN_ROW
    for g in range(N_HEADS // per):
        rows = [acc_ref[h, 0:DEN_ROW, :] / acc_ref[h, DEN_ROW:DEN_ROW + 1, :]
                for h in range(g * per, (g + 1) * per)]
        o_ref[:, g * LANES:(g + 1) * LANES] = jnp.concatenate(rows, axis=0).T.astype(o_ref.dtype)


def _flash(qt, k, vt, *, mode, tq):
    b, nh, s, w = k.shape
    nq = s // tq
    assert tq % ROWS == 0 and qt.shape == vt.shape == (b, nq, nh, w, tq)
    return pl.pallas_call(
        functools.partial(_flash_kernel, mode=mode, tq=tq),
        grid=(b, nq),
        in_specs=[pl.BlockSpec((None, None, nh, w, tq), lambda bi, i: (bi, i, 0, 0, 0)),
                  pl.BlockSpec((None, nh, s, w), lambda bi, i: (bi, 0, 0, 0)),
                  pl.BlockSpec((None, nq, nh, w, tq), lambda bi, i: (bi, 0, 0, 0, 0))],
        out_specs=pl.BlockSpec((None, tq, nh * DEN_ROW), lambda bi, i: (bi, i, 0)),
        out_shape=jax.ShapeDtypeStruct((b, s, nh * DEN_ROW), BF16),
        scratch_shapes=[pltpu.VMEM((nh, 1, tq), F32),
                        pltpu.VMEM((nh, 1, tq), F32),
                        pltpu.VMEM((nh, w, tq), F32),
                        pltpu.VMEM((2, tq, tq), F32)],
        compiler_params=_params(("parallel", "arbitrary")),
        name="flash_" + mode,
    )(qt, k, vt)


def _merge_kernel(x_ref, of_ref, og_ref, om_ref, g_ref, wz_ref, bz_ref, wf_ref, wgl_ref, wm_ref,
                  wo_ref, o_ref):
    x = x_ref[...]
    d = x.shape[-1]
    h = _rms(x, g_ref[...]).astype(BF16)
    y = None
    for br, (b_ref, w_ref) in enumerate(((of_ref, wf_ref), (og_ref, wgl_ref), (om_ref, wm_ref))):
        cs = slice(br * d, (br + 1) * d)
        gate = jax.nn.sigmoid(_dot(h, wz_ref[:, cs]) + bz_ref[:, cs])
        term = gate * _dot(b_ref[...], w_ref[...])
        y = term if y is None else y + term
    o_ref[...] = x + _dot(y.astype(BF16), wo_ref[...])


def _merge(x2d, of, og, om, g, wz, bz, wf, wgl, wm, wo, tm):
    t, d = x2d.shape
    row = lambda w: pl.BlockSpec((tm, w), lambda i: (i, 0))
    return pl.pallas_call(
        _merge_kernel,
        grid=(t // tm,),
        in_specs=[row(d), row(of.shape[1]), row(og.shape[1]), row(om.shape[1]), _full(g.shape),
                  _full(wz.shape), _full(bz.shape), _full(wf.shape), _full(wgl.shape),
                  _full(wm.shape), _full(wo.shape)],
        out_specs=row(d),
        out_shape=jax.ShapeDtypeStruct((t, d), F32),
        compiler_params=_params(("parallel",)),
        name="merge",
    )(x2d, of, og, om, g, wz, bz, wf, wgl, wm, wo)


def _memkv_kernel(m_ref, g_ref, w_ref, k_ref, v_ref):
    h = _rms(m_ref[...], g_ref[...]).astype(BF16)
    kv = _dot(h, w_ref[...])
    w = k_ref.shape[-1]
    k_ref[...] = kv[:, :w].astype(BF16)
    v_ref[...] = kv[:, w:].T.astype(BF16)


def _memkv(mem2d, g, w, tm):
    t, d = mem2d.shape
    xw = w.shape[1] // 2
    return pl.pallas_call(
        _memkv_kernel,
        grid=(t // tm,),
        in_specs=[pl.BlockSpec((tm, d), lambda i: (i, 0)), _full(g.shape), _full(w.shape)],
        out_specs=[pl.BlockSpec((tm, xw), lambda i: (i, 0)), pl.BlockSpec((xw, tm), lambda i: (0, i))],
        out_shape=[jax.ShapeDtypeStruct((t, xw), BF16), jax.ShapeDtypeStruct((xw, t), BF16)],
        compiler_params=_params(("parallel",)),
        name="memkv",
    )(mem2d, g, w)


def _xattn_kernel(x_ref, k_ref, vt_ref, g_ref, wq_ref, wo_ref, o_ref):
    x = x_ref[...]
    h = _rms(x, g_ref[...]).astype(BF16)
    q = _dot(h, wq_ref[...]).astype(BF16)
    heads = [slice(hd * XA_HD, (hd + 1) * XA_HD) for hd in range(XA_HEADS)]
    scores = [_dot_nt(k_ref[:, sl], q[:, sl]) * (XA_HD ** -0.5) for sl in heads]
    probs = []
    for st in scores:
        e = jnp.exp(st - jnp.max(st, axis=0, keepdims=True))
        probs.append((e / jnp.sum(e, axis=0, keepdims=True)).astype(BF16))
    outs = [_dot(vt_ref[sl, :], p) for sl, p in zip(heads, probs)]
    o = jnp.concatenate(outs, axis=0).T.astype(BF16)
    o_ref[...] = x + _dot(o, wo_ref[...])


def _xattn(x, k, vt, g, wq, wo, tm):
    b, s, d = x.shape
    m, xw = k.shape[1], k.shape[2]
    return pl.pallas_call(
        _xattn_kernel,
        grid=(b, s // tm),
        in_specs=[pl.BlockSpec((None, tm, d), lambda bi, i: (bi, i, 0)),
                  pl.BlockSpec((None, m, xw), lambda bi, i: (bi, 0, 0)),
                  pl.BlockSpec((xw, m), lambda bi, i: (0, bi)),
                  _full(g.shape), _full(wq.shape), _full(wo.shape)],
        out_specs=pl.BlockSpec((None, tm, d), lambda bi, i: (bi, i, 0)),
        out_shape=jax.ShapeDtypeStruct(x.shape, F32),
        compiler_params=_params(("parallel", "parallel")),
        name="xattn",
    )(x, k, vt, g, wq, wo)


def _mlp_kernel(x_ref, g_ref, w1_ref, w2_ref, gf_ref, o_ref, *, ff_tile, final_norm):
    x = x_ref[...]
    h = _rms(x, g_ref[...]).astype(BF16)
    acc = x
    for f0 in range(0, w1_ref.shape[1], ff_tile):
        a = jnp.square(jnp.maximum(_dot(h, w1_ref[:, f0:f0 + ff_tile]), 0.0))
        acc = acc + _dot(a.astype(BF16), w2_ref[f0:f0 + ff_tile, :])
    o_ref[...] = _rms(acc, gf_ref[...]) if final_norm else acc


def _mlp(x2d, g, w1, w2, gf, tm, final_norm):
    t, d = x2d.shape
    row = pl.BlockSpec((tm, d), lambda i: (i, 0))
    return pl.pallas_call(
        functools.partial(_mlp_kernel, ff_tile=min(1024, w1.shape[1]), final_norm=final_norm),
        grid=(t // tm,),
        in_specs=[row, _full(g.shape), _full(w1.shape), _full(w2.shape), _full(gf.shape)],
        out_specs=row,
        out_shape=jax.ShapeDtypeStruct((t, d), F32),
        compiler_params=_params(("parallel",)),
        name="mlp",
    )(x2d, g, w1, w2, gf)


def _slot_cols(w, heads, hd):
    k = w.shape[0]
    w = w.reshape(k, heads, hd)
    return jnp.pad(w, ((0, 0), (0, 0), (0, LANES - hd))).reshape(k, heads * LANES)


REPACK_PIECES = 3


def _repack_kernel(blk_ref, par_ref, a_ref, b_ref, o_ref, *, n_cols):
    i = pl.program_id(0)
    n_layers, k = a_ref.shape[1], a_ref.shape[2]
    row = blk_ref[i] * LANES + lax.broadcasted_iota(jnp.int32, (2 * LANES, k), 0)
    j = lax.broadcasted_iota(jnp.int32, (LANES, 2 * LANES), 0)
    r = lax.broadcasted_iota(jnp.int32, (LANES, 2 * LANES), 1)
    sel = None
    for p in range(REPACK_PIECES):
        off = par_ref[(i * REPACK_PIECES + p) * 3]
        width = par_ref[(i * REPACK_PIECES + p) * 3 + 1]
        dst = par_ref[(i * REPACK_PIECES + p) * 3 + 2]
        hit = (r - off == j - dst) & (j >= dst) & (j < dst + width)
        sel = hit if sel is None else sel | hit
    sel = sel.astype(BF16)
    for l in range(n_layers):
        window = jnp.concatenate([a_ref[:, l, :], b_ref[:, l, :]], axis=0)
        window = jnp.where(row < n_cols, window, 0.0).astype(BF16)
        o_ref[l] = _dot(sel, window).T.astype(BF16)


def _repack(w_all, blocks):
    n_layers, k, n_cols = w_all.shape
    last_blk = (n_cols - 1) // LANES
    blk, par = [], []
    for pieces in blocks:
        b0 = min(p[0] for p in pieces) // LANES
        assert len(pieces) <= REPACK_PIECES
        pieces = list(pieces) + [(b0 * LANES, 0, 0)] * (REPACK_PIECES - len(pieces))
        for src, width, dst in pieces:
            assert 0 <= src - b0 * LANES and src - b0 * LANES + width <= 2 * LANES and dst + width <= LANES
            par += [src - b0 * LANES, width, dst]
        blk.append(b0)
    w_t = jnp.transpose(w_all, (2, 0, 1))
    grid_spec = pltpu.PrefetchScalarGridSpec(
        num_scalar_prefetch=2,
        grid=(len(blocks),),
        in_specs=[pl.BlockSpec((LANES, n_layers, k), lambda i, blk, par: (blk[i], 0, 0)),
                  pl.BlockSpec((LANES, n_layers, k),
                               lambda i, blk, par: (jnp.minimum(blk[i] + 1, last_blk), 0, 0))],
        out_specs=pl.BlockSpec((n_layers, k, LANES), lambda i, blk, par: (0, 0, i)),
    )
    return pl.pallas_call(
        functools.partial(_repack_kernel, n_cols=n_cols),
        grid_spec=grid_spec,
        out_shape=jax.ShapeDtypeStruct((n_layers, k, len(blocks) * LANES), BF16),
        compiler_params=_params(("arbitrary",)),
        name="repack",
    )(jnp.asarray(np.array(blk, np.int32)), jnp.asarray(np.array(par, np.int32)), w_t, w_t)


def _inproj_weights(w_all):
    d = w_all.shape[1]
    fw, kw, vw = FOX_HEADS * FOX_HD, GLA_HEADS * GLA_DK, GLA_HEADS * GLA_DV
    sizes = (fw, fw, fw, FOX_HEADS, kw, kw, vw, GLA_GATE_RANK, vw,
             MLA_Q_RANK, MLA_KV_RANK, MLA_ROPE, N_BRANCH * d)
    starts = np.concatenate([[0], np.cumsum(sizes)]).tolist()
    fq, fk, fv, ff, gq, gk, gv, glow, gr, mq, mkv, mkr, zg = starts[:-1]
    half = MLA_ROPE // 2
    dense = lambda c0, width: [[(c0 + i, LANES, 0)] for i in range(0, width, LANES)]
    blocks = (dense(fq, fw) + dense(fk, fw) + dense(fv, fw)
              + dense(gq, kw) + dense(gk, kw) + dense(gv, vw) + dense(gr, vw)
              + [[(ff, FOX_HEADS, p * FOX_HEADS) for p in range(N_PIECES)],
                 [(glow, GLA_GATE_RANK, 0)], [(mkr, MLA_ROPE, ROPE_LANE)],
                 [(mkr + half, half, ROPE_LANE), (mkr, half, ROPE_LANE + half)]]
              + dense(mq, MLA_Q_RANK) + dense(mkv, MLA_KV_RANK))
    n, tr = False, True
    outs = [(fw, BF16, n), (fw, BF16, n), (fw, BF16, n), (kw, BF16, n), (kw, F32, n),
            (vw, BF16, tr), (vw, F32, n), (LANES, F32, n), (LANES, F32, n), (LANES, F32, n),
            (LANES, F32, n), (MLA_Q_RANK, F32, n), (MLA_KV_RANK, F32, n)]
    return _repack(w_all, blocks), outs, _repack(w_all, dense(zg, N_BRANCH * d))


def _mla_weights(w_uq, w_ukv):
    half = MLA_ROPE // 2
    qk = MLA_NOPE + MLA_ROPE
    r = w_uq.shape[0]
    wq = w_uq.reshape(r, MLA_HEADS, qk)
    nope, x1, x2 = wq[..., :MLA_NOPE], wq[..., MLA_NOPE:MLA_NOPE + half], wq[..., MLA_NOPE + half:]
    pad = jnp.zeros((r, MLA_HEADS, LANES - qk), F32)
    wqa = jnp.concatenate([nope, x1, x2, pad], axis=-1).reshape(r, SLOTS)
    wqb = jnp.concatenate([jnp.zeros_like(nope), x2, x1, pad], axis=-1).reshape(r, SLOTS)
    rk = w_ukv.shape[0]
    wkv = w_ukv.reshape(rk, MLA_HEADS, MLA_NOPE + MLA_VD)
    wk = _slot_cols(wkv[..., :MLA_NOPE].reshape(rk, -1), MLA_HEADS, MLA_NOPE)
    wv = _slot_cols(wkv[..., MLA_NOPE:].reshape(rk, -1), MLA_HEADS, MLA_VD)
    return wqa.astype(BF16), wqb.astype(BF16), wk.astype(BF16), wv.astype(BF16)


def _rope_tables(seq):
    half = MLA_ROPE // 2
    inv = ROPE_BASE ** (-jnp.arange(half, dtype=F32) / half)
    ang = jnp.arange(seq).astype(F32)[:, None] * inv[None, :]
    cos, sin = jnp.cos(ang), jnp.sin(ang)
    one = jnp.ones((seq, MLA_NOPE), F32)
    zero = jnp.zeros((seq, LANES - MLA_NOPE - MLA_ROPE), F32)
    ct = jnp.concatenate([one, cos, cos, zero], axis=1)
    st = jnp.concatenate([jnp.zeros_like(one), -sin, sin, zero], axis=1)
    return ct, st


def _tile(n, pref):
    t = min(n, pref)
    assert n % t == 0, (n, t)
    return t


def kernel(x, mem, g_mix, w_in, b_fox_forget, w_gla_gate, b_gla_gate, g_gla_out, g_mla_q, w_mla_uq, g_mla_kv, w_mla_ukv, b_branch_gate, w_up_fox, w_up_gla, w_up_mla, w_out, g_xa, g_mem, w_xq, w_xkv, w_xo, g_mlp, w_mlp1, w_mlp2, g_final):
    b, s, d = x.shape
    depth = w_in.shape[0]
    t = b * s
    mlen = mem.shape[1]
    tm = _tile(t, 1024)
    ts = _tile(s, 512)
    tq = _tile(s, 512)
    ct, st = _rope_tables(s)
    x2d = x.reshape(t, d)
    mem2d = mem.reshape(b * mlen, d)
    row = lambda v: v.reshape(1, -1)
    w1_all, outs, wz_all = _inproj_weights(w_in)
    for l in range(depth):
        w1, wz = w1_all[l], wz_all[l]
        fq, fk, fv, gq, gk, gv, gr, ff, glow, kr, krs, cq, ckv = _inproj(x2d, row(g_mix[l]), w1, outs, tm)
        as3 = lambda a: a.reshape(b, s, a.shape[-1])
        bf = jnp.pad(jnp.tile(b_fox_forget[l], N_PIECES), (0, LANES - N_PIECES * FOX_HEADS)).reshape(1, LANES)
        wqa, wqb, wk, wv = _mla_weights(w_mla_uq[l], w_mla_ukv[l])
        (fqt, fka, fvt), (mqt, mk, mvt) = _prep(
            as3(fq), as3(fk), as3(fv), as3(ff), bf, cq, ckv, kr, krs, row(g_mla_q[l]),
            row(g_mla_kv[l]), wqa, wqb, wk, wv, ct, st, ts)
        o_fox = _flash(fqt, fka, fvt, mode="causal", tq=tq)
        o_mla = _flash(mqt, mk, mvt, mode="chunk", tq=tq)
        wg = jnp.pad(w_gla_gate[l], ((0, LANES - GLA_GATE_RANK), (0, 0))).astype(BF16)
        o_gla = _gla(as3(gq), as3(gk), gv, as3(gr), as3(glow), wg, row(b_gla_gate[l]),
                     row(g_gla_out[l]), _tile(s, 1024))
        x2d = _merge(x2d, o_fox.reshape(t, -1), o_gla.reshape(t, -1), o_mla.reshape(t, -1),
                     row(g_mix[l]), wz, row(b_branch_gate[l]), w_up_fox[l].astype(BF16),
                     w_up_gla[l].astype(BF16), w_up_mla[l].astype(BF16), w_out[l].astype(BF16), tm)
        km, vmt = _memkv(mem2d, row(g_mem[l]), w_xkv[l].astype(BF16), _tile(b * mlen, 512))
        x2d = _xattn(x2d.reshape(b, s, d), km.reshape(b, mlen, -1), vmt,
                     row(g_xa[l]), w_xq[l].astype(BF16), w_xo[l].astype(BF16),
                     _tile(s, 1024)).reshape(t, d)
        x2d = _mlp(x2d, row(g_mlp[l]), w_mlp1[l].astype(BF16), w_mlp2[l].astype(BF16),
                   row(g_final), tm, final_norm=(l == depth - 1))
    return x2d.reshape(b, s, d)
```

```python
import functools

import numpy as np
import jax
import jax.numpy as jnp
from jax import lax
from jax.experimental import pallas as pl
from jax.experimental.pallas import tpu as pltpu

F32 = jnp.float32
BF16 = jnp.bfloat16

CHUNK = 64
EPS = 1e-6
FOX_HEADS, FOX_HD = 4, 64
GLA_HEADS, GLA_DK, GLA_DV, GLA_GATE_RANK, GLA_TAU = 4, 64, 128, 16, 16.0
MLA_HEADS, MLA_Q_RANK, MLA_KV_RANK, MLA_NOPE, MLA_ROPE, MLA_VD = 4, 256, 128, 64, 32, 64
ROPE_BASE = 10000.0
XA_HEADS, XA_HD = 4, 128
N_BRANCH = 3

LANES = 128
MXU_DEPTH = 256
VMEM_LIMIT = 56 * 1024 * 1024

N_HEADS = 4
SLOTS = N_HEADS * LANES
MASK_VALUE = -1e30

DEC_LANE = FOX_HD
ROPE_LANE = MLA_NOPE
DEN_ROW = 64
assert DEN_ROW == FOX_HD == MLA_VD
LOG2E = 1.4426950408889634
MLA_SCALE = (MLA_NOPE + MLA_ROPE) ** -0.5


def _params(sem):
    return pltpu.CompilerParams(dimension_semantics=sem, vmem_limit_bytes=VMEM_LIMIT)


def _dot(a, b):
    return jnp.dot(a, b, preferred_element_type=F32)


def _dot_nt(a, b):
    return lax.dot_general(a, b, (((1,), (1,)), ((), ())), preferred_element_type=F32)


def _dot_tn(a, b):
    return lax.dot_general(a, b, (((0,), (0,)), ((), ())), preferred_element_type=F32)


def _rms(x, g):
    y = x * lax.rsqrt(jnp.mean(x * x, axis=-1, keepdims=True) + EPS)
    return y * g


def _log_sigmoid(x):
    return -(jnp.maximum(-x, 0.0) + jnp.log1p(jnp.exp(-jnp.abs(x))))


N_PIECES = 3


def _split3(x):
    p1 = x.astype(BF16)
    r1 = x - p1.astype(F32)
    p2 = r1.astype(BF16)
    r2 = r1 - p2.astype(F32)
    return p1, p2, r2.astype(BF16)


def _lower_tri(n):
    r = lax.broadcasted_iota(jnp.int32, (n, n), 0)
    c = lax.broadcasted_iota(jnp.int32, (n, n), 1)
    return (r >= c).astype(BF16)


def _cumsum_rows(tri, pieces):
    return _dot(tri, pieces[0]) + _dot(tri, pieces[1]) + _dot(tri, pieces[2])


def _div_pow2(x, n):
    assert n & (n - 1) == 0, n
    return jnp.right_shift(x, n.bit_length() - 1)


def _full(shape):
    return pl.BlockSpec(shape, lambda *_: (0,) * len(shape), pipeline_mode=pl.Buffered(1))


INPROJ_CHUNK = 2 * MXU_DEPTH


def _inproj_kernel(x_ref, g_ref, w_ref, *out_refs, cols):
    h = _rms(x_ref[...], g_ref[...]).astype(BF16)
    n = w_ref.shape[1]
    for c0 in range(0, n, INPROJ_CHUNK):
        c1 = min(c0 + INPROJ_CHUNK, n)
        y = _dot(h, w_ref[:, c0:c1])
        for o_ref, (o0, ow, transposed) in zip(out_refs, cols):
            lo, hi = max(o0, c0), min(o0 + ow, c1)
            if lo >= hi:
                continue
            piece = y[:, lo - c0:hi - c0]
            if transposed:
                o_ref[lo - o0:hi - o0, :] = piece.T.astype(o_ref.dtype)
            else:
                o_ref[:, lo - o0:hi - o0] = piece.astype(o_ref.dtype)


def _inproj(x2d, g, w, outs, tm):
    t, d = x2d.shape
    cols, c0 = [], 0
    for cw, _, tr in outs:
        cols.append((c0, cw, tr))
        c0 += cw
    return pl.pallas_call(
        functools.partial(_inproj_kernel, cols=tuple(cols)),
        grid=(t // tm,),
        in_specs=[pl.BlockSpec((tm, d), lambda i: (i, 0)), _full((1, d)), _full(w.shape)],
        out_specs=[pl.BlockSpec((cw, tm), lambda i: (0, i)) if tr else pl.BlockSpec((tm, cw), lambda i: (i, 0))
                   for cw, _, tr in outs],
        out_shape=[jax.ShapeDtypeStruct((cw, t) if tr else (t, cw), dt) for cw, dt, tr in outs],
        compiler_params=_params(("parallel",)),
        name="inproj",
    )(x2d, g, w)


def _fox_prep_kernel(fq_ref, fk_ref, fv_ref, ff_ref, bf_ref, eqt_ref, ek_ref, es_ref, qo_ref, ko_ref,
                     vo_ref, carry_ref, *, ts):
    @pl.when(pl.program_id(1) == 0)
    def _():
        carry_ref[...] = jnp.zeros_like(carry_ref)

    lane = lax.broadcasted_iota(jnp.int32, (ts, LANES), 1)
    live = lane < N_PIECES * FOX_HEADS
    z = jnp.where(live, ff_ref[...] + bf_ref[...], 0.0)
    lf = jnp.where(live, _log_sigmoid(z), 0.0)
    cum3 = _dot(_lower_tri(ts), jnp.concatenate(_split3(lf), axis=1))
    cum = cum3[:, :LANES] + cum3[:, LANES:2 * LANES] + cum3[:, 2 * LANES:] + carry_ref[0:1, :]
    carry_ref[...] = jnp.broadcast_to(cum[ts - 1:ts, :], carry_ref.shape)
    p1, p2, p3 = _split3(cum * LOG2E)
    one = jnp.where(lane == N_PIECES * FOX_HEADS, 1.0, 0.0).astype(BF16)
    pieces = jnp.where(lane < FOX_HEADS, p1, jnp.where(lane < 2 * FOX_HEADS, p2,
                       jnp.where(lane < 3 * FOX_HEADS, p3, one)))
    q_t = fq_ref[...].astype(F32).T * (FOX_HD ** -0.5 * LOG2E)
    v_t = fv_ref[...].astype(F32).T
    aug_qt = _dot(eqt_ref[...], pieces.astype(F32).T.astype(BF16))
    k_slots = _dot(fk_ref[...], es_ref[...]) + _dot(pieces, ek_ref[...])
    den_rows = jnp.where(lax.broadcasted_iota(jnp.int32, (LANES - FOX_HD, ts), 0) == 0, 1.0, 0.0)
    for h in range(FOX_HEADS):
        ch = slice(h * FOX_HD, (h + 1) * FOX_HD)
        spare = slice(h * LANES + FOX_HD, (h + 1) * LANES)
        qo_ref[h] = jnp.concatenate([q_t[ch], aug_qt[spare]], axis=0).astype(BF16)
        ko_ref[h] = k_slots[:, h * LANES:(h + 1) * LANES].astype(BF16)
        vo_ref[h] = jnp.concatenate([v_t[ch], den_rows], axis=0).astype(BF16)


def _fox_placement():
    eq = np.zeros((LANES, SLOTS), np.float32)
    ek = np.zeros((LANES, SLOTS), np.float32)
    one_lane = N_PIECES * FOX_HEADS
    for h in range(FOX_HEADS):
        base = h * LANES + DEC_LANE
        for p in range(N_PIECES):
            eq[p * FOX_HEADS + h, base + p] = 1.0
            eq[one_lane, base + N_PIECES + p] = 1.0
            ek[one_lane, base + p] = 1.0
            ek[p * FOX_HEADS + h, base + N_PIECES + p] = -1.0
    es = np.zeros((FOX_HEADS * FOX_HD, SLOTS), np.float32)
    for h in range(FOX_HEADS):
        for c in range(FOX_HD):
            es[h * FOX_HD + c, h * LANES + c] = 1.0
    return jnp.asarray(eq.T, BF16), jnp.asarray(ek, BF16), jnp.asarray(es, BF16)


def _head_major_specs(batch, seq, ts):
    spec_t = pl.BlockSpec((None, None, N_HEADS, LANES, ts), lambda bi, i: (bi, i, 0, 0, 0))
    spec_n = pl.BlockSpec((None, N_HEADS, ts, LANES), lambda bi, i: (bi, 0, i, 0))
    shape_t = jax.ShapeDtypeStruct((batch, seq // ts, N_HEADS, LANES, ts), BF16)
    shape_n = jax.ShapeDtypeStruct((batch, N_HEADS, seq, LANES), BF16)
    return spec_t, spec_n, shape_t, shape_n


N_FOX_IN, N_MLA_IN, N_PREP_OUT = 8, 12, 3


def _prep_kernel(*refs, ts):
    fox_in = refs[:N_FOX_IN]
    mla_in = refs[N_FOX_IN:N_FOX_IN + N_MLA_IN]
    outs = refs[N_FOX_IN + N_MLA_IN:N_FOX_IN + N_MLA_IN + 2 * N_PREP_OUT]
    carry_ref = refs[-1]
    _fox_prep_kernel(*fox_in, *outs[:N_PREP_OUT], carry_ref, ts=ts)
    _mla_prep_kernel(*mla_in, *outs[N_PREP_OUT:], ts=ts)


def _prep(fq, fk, fv, ff, bf, cq, ckv, kr, krs, gq, gkv, wqa, wqb, wk, wv, ct, st, ts):
    b, s, w = fq.shape
    nblk = s // ts
    spec = pl.BlockSpec((None, ts, w), lambda bi, i: (bi, i, 0))
    row = lambda width: pl.BlockSpec((ts, width), lambda bi, i: (bi * nblk + i, 0))
    tab = pl.BlockSpec((ts, LANES), lambda bi, i: (i, 0))
    spec_t, spec_n, shape_t, shape_n = _head_major_specs(b, s, ts)
    eq, ek, es = _fox_placement()
    fox_specs = [spec, spec, spec, pl.BlockSpec((None, ts, LANES), lambda bi, i: (bi, i, 0)),
                 _full((1, LANES)), _full(eq.shape), _full(ek.shape), _full(es.shape)]
    mla_specs = [row(MLA_Q_RANK), row(MLA_KV_RANK), row(LANES), row(LANES), _full(gq.shape),
                 _full(gkv.shape), _full(wqa.shape), _full(wqb.shape), _full(wk.shape),
                 _full(wv.shape), tab, tab]
    assert len(fox_specs) == N_FOX_IN and len(mla_specs) == N_MLA_IN
    outs = pl.pallas_call(
        functools.partial(_prep_kernel, ts=ts),
        grid=(b, nblk),
        in_specs=fox_specs + mla_specs,
        out_specs=[spec_t, spec_n, spec_t] * 2,
        out_shape=[shape_t, shape_n, shape_t] * 2,
        scratch_shapes=[pltpu.VMEM((8, LANES), F32)],
        compiler_params=_params(("parallel", "arbitrary")),
        name="prep",
    )(fq, fk, fv, ff, bf, eq, ek, es, cq, ckv, kr, krs, gq, gkv, wqa, wqb, wk, wv, ct, st)
    return outs[:N_PREP_OUT], outs[N_PREP_OUT:]


def _mla_prep_kernel(cq_ref, ckv_ref, kr_ref, krs_ref, gq_ref, gkv_ref, wqa_ref, wqb_ref,
                     wk_ref, wv_ref, ct_ref, st_ref, q_out, k_out, v_out, *, ts):
    hq = _rms(cq_ref[...], gq_ref[...]).astype(BF16)
    hkv = _rms(ckv_ref[...], gkv_ref[...]).astype(BF16)
    ct = ct_ref[...]
    st = st_ref[...]
    lane = lax.broadcasted_iota(jnp.int32, (ts, LANES), 1)
    in_rope = (lane >= ROPE_LANE) & (lane < ROPE_LANE + MLA_ROPE)
    k_rope = jnp.where(in_rope, kr_ref[...] * ct + krs_ref[...] * st, 0.0)
    qa = _dot(hq, wqa_ref[...])
    qb = _dot(hq, wqb_ref[...])
    kk = _dot(hkv, wk_ref[...])
    vv = _dot(hkv, wv_ref[...])
    for h in range(MLA_HEADS):
        sl = slice(h * LANES, (h + 1) * LANES)
        q_out[h] = ((qa[:, sl] * ct + qb[:, sl] * st) * (MLA_SCALE * LOG2E)).T.astype(BF16)
        k_out[h] = (kk[:, sl] + k_rope).astype(BF16)
        v_out[h] = jnp.where(lane == DEN_ROW, 1.0, vv[:, sl]).T.astype(BF16)


def _gla_kernel(gq_ref, gk_ref, gvt_ref, gr_ref, glow_ref, wg_ref, bg_ref, go_ref, o_ref,
                state_ref, *, tc):
    kw = GLA_HEADS * GLA_DK
    vw = GLA_HEADS * GLA_DV
    assert 2 * GLA_DK == LANES and GLA_DV == LANES

    @pl.when(pl.program_id(1) == 0)
    def _():
        state_ref[...] = jnp.zeros_like(state_ref)

    first_of_pair = lax.broadcasted_iota(jnp.int32, (GLA_DV, LANES), 1) < GLA_DK
    q_first = lax.broadcasted_iota(jnp.int32, (CHUNK, LANES), 1) < GLA_DK
    gr_rows = MXU_DEPTH
    ri = lax.broadcasted_iota(jnp.int32, (gr_rows, gr_rows), 0)
    ci = lax.broadcasted_iota(jnp.int32, (gr_rows, gr_rows), 1)
    tri = ((_div_pow2(ri, CHUNK) == _div_pow2(ci, CHUNK)) & (ri >= ci)).astype(BF16)
    chunk_of_row = _div_pow2(lax.broadcasted_iota(jnp.int32, (gr_rows, kw), 0), CHUNK)
    g_out = go_ref[...]
    pairs = range(GLA_HEADS // 2)
    chunks_per_group = gr_rows // CHUNK
    ends, k_decs, qs = [], [], []
    for g0 in range(0, tc, gr_rows):
        gs = slice(g0, g0 + gr_rows)
        gate = _dot(glow_ref[gs, :].astype(BF16), wg_ref[...]) + bg_ref[...]
        cum = _cumsum_rows(tri, _split3(_log_sigmoid(gate) * (1.0 / GLA_TAU)))
        g_ends = [cum[(c + 1) * CHUNK - 1:(c + 1) * CHUNK, :] for c in range(chunks_per_group)]
        end_rows = jnp.concatenate([jnp.broadcast_to(e, (CHUNK, kw)) for e in g_ends], axis=0)
        ends += g_ends
        k_decs.append(gk_ref[gs, :] * jnp.exp(end_rows - cum))
        qs.append((gq_ref[gs, :].astype(F32) * (GLA_DK ** -0.5)).astype(BF16))
    u_pairs = []
    for n in range(tc // CHUNK):
        g, c = divmod(n, chunks_per_group)
        k_c = jnp.where(chunk_of_row == c, k_decs[g], 0.0).astype(BF16)
        u_t = _dot(gvt_ref[:, g * gr_rows:(g + 1) * gr_rows], k_c)
        u_pairs.append([jnp.where(first_of_pair,
                                  u_t[2 * p * GLA_DV:(2 * p + 1) * GLA_DV, p * LANES:(p + 1) * LANES],
                                  u_t[(2 * p + 1) * GLA_DV:(2 * p + 2) * GLA_DV, p * LANES:(p + 1) * LANES])
                        for p in pairs])
    states = [state_ref[p] for p in pairs]
    chunk_states = []
    for n in range(tc // CHUNK):
        a = jnp.exp(ends[n])
        states = [a[:, p * LANES:(p + 1) * LANES] * states[p] + u_pairs[n][p] for p in pairs]
        chunk_states.append([st.astype(BF16) for st in states])
    for p in pairs:
        state_ref[p] = states[p]
    for n in range(tc // CHUNK):
        g, c = divmod(n, chunks_per_group)
        rows = slice(n * CHUNK, (n + 1) * CHUNK)
        for p in pairs:
            q_pair = qs[g][c * CHUNK:(c + 1) * CHUNK, p * LANES:(p + 1) * LANES]
            for half in range(2):
                h = 2 * p + half
                q_h = jnp.where(q_first if half == 0 else jnp.logical_not(q_first), q_pair, 0.0)
                o = _dot_nt(q_h.astype(BF16), chunk_states[n][p])
                sl = slice(h * GLA_DV, (h + 1) * GLA_DV)
                r = gr_ref[rows, sl]
                o_ref[rows, sl] = (_rms(o, g_out) * (r * jax.nn.sigmoid(r))).astype(BF16)


def _gla(gq, gk, gvt, gr, glow, wg, bg, go, tc):
    b, s, _ = gq.shape
    nblk = s // tc
    spec = lambda w: pl.BlockSpec((None, tc, w), lambda bi, i: (bi, i, 0))
    kw, vw = GLA_HEADS * GLA_DK, GLA_HEADS * GLA_DV
    return pl.pallas_call(
        functools.partial(_gla_kernel, tc=tc),
        grid=(b, nblk),
        in_specs=[spec(kw), spec(kw), pl.BlockSpec((vw, tc), lambda bi, i: (0, bi * nblk + i)),
                  spec(vw), spec(LANES), _full(wg.shape), _full(bg.shape), _full(go.shape)],
        out_specs=spec(vw),
        out_shape=jax.ShapeDtypeStruct((b, s, vw), BF16),
        scratch_shapes=[pltpu.VMEM((GLA_HEADS // 2, GLA_DV, LANES), F32)],
        compiler_params=_params(("parallel", "arbitrary")),
        name="gla",
    )(gq, gk, gvt, gr, glow, wg, bg, go)


ROWS = MXU_DEPTH
V_ROWS = 80
assert V_ROWS > DEN_ROW and V_ROWS % 16 == 0


def _flash_kernel(qt_ref, k_ref, vt_ref, o_ref, m_ref, alpha_ref, acc_ref, s_ref, *, mode, tq):
    i = pl.program_id(1)
    sub = tq
    assert sub % CHUNK == 0
    m_ref[...] = jnp.full_like(m_ref, MASK_VALUE)
    acc_ref[...] = jnp.zeros_like(acc_ref)

    def score_chunk(base, item, buf, r, m_run):
        u, h, masked = item
        k0 = pl.multiple_of((base + u) * sub + r, ROWS)
        q0 = r if masked else 0
        s = _dot(k_ref[h, pl.ds(k0, ROWS), :], qt_ref[h, :, q0:])
        if masked:
            s_pos = k0 + lax.broadcasted_iota(jnp.int32, (ROWS, tq - q0), 0)
            t_pos = i * tq + q0 + lax.broadcasted_iota(jnp.int32, (ROWS, tq - q0), 1)
            if mode == "causal":
                keep = s_pos <= t_pos
            else:
                keep = _div_pow2(s_pos, CHUNK) <= _div_pow2(t_pos, CHUNK)
            s = jnp.where(keep, s, MASK_VALUE)
        s_ref[buf, r:r + ROWS, q0:] = s
        m_new = jnp.maximum(m_run[:, q0:], jnp.max(s, axis=0, keepdims=True))
        return m_new if q0 == 0 else jnp.concatenate([m_run[:, :q0], m_new], axis=1)

    def value_chunk(base, item, buf, r, m_new):
        u, h, masked = item
        q0 = r if masked else 0
        p = jnp.exp2(s_ref[buf, r:r + ROWS, q0:] - m_new[:, q0:]).astype(BF16)
        d = _dot(vt_ref[base + u, h, 0:V_ROWS, r:r + ROWS], p)
        return d if q0 == 0 else jnp.concatenate([jnp.zeros((V_ROWS, q0), F32), d], axis=1)

    def finish_scores(h, m_prev, m_new):
        alpha_ref[h] = jnp.exp2(m_prev - m_new)
        m_ref[h] = m_new

    def body(base, sub_masked):
        items = [(u, h, m) for u, m in enumerate(sub_masked) for h in range(N_HEADS)]
        chunks = range(0, sub, ROWS)
        h0 = items[0][1]
        m_prev = m_ref[h0]
        m_run = m_prev
        for r in chunks:
            m_run = score_chunk(base, items[0], 0, r, m_run)
        finish_scores(h0, m_prev, m_run)
        for n, item in enumerate(items):
            h = item[1]
            nxt = items[n + 1] if n + 1 < len(items) else None
            m_new = m_ref[h]
            if nxt is not None:
                m_prev = m_ref[nxt[1]]
                m_run = m_prev
            pv = None
            for r in chunks:
                if nxt is not None:
                    m_run = score_chunk(base, nxt, (n + 1) % 2, r, m_run)
                d = value_chunk(base, item, n % 2, r, m_new)
                pv = d if pv is None else pv + d
            if nxt is not None:
                finish_scores(nxt[1], m_prev, m_run)
            acc_ref[h, 0:V_ROWS, :] = alpha_ref[h] * acc_ref[h, 0:V_ROWS, :] + pv

    def full_pair(t, carry):
        body(2 * t, (False, False))
        return carry

    lax.fori_loop(0, jnp.right_shift(i, 1), full_pair, 0)

    @pl.when(jnp.bitwise_and(i, 1) == 1)
    def _():
        body(i - 1, (False, True))

    @pl.when(jnp.bitwise_and(i, 1) == 0)
    def _():
        body(i, (True,))

    per = LANES // DEN_ROW
    for g in range(N_HEADS // per):
        rows = [acc_ref[h, 0:DEN_ROW, :] / acc_ref[h, DEN_ROW:DEN_ROW + 1, :]
                for h in range(g * per, (g + 1) * per)]
        o_ref[:, g * LANES:(g + 1) * LANES] = jnp.concatenate(rows, axis=0).T.astype(o_ref.dtype)


def _flash_pair_kernel(qt1, k1, vt1, qt2, k2, vt2, o1, o2, m_ref, alpha_ref, acc_ref, s_ref, *, tq):
    _flash_kernel(qt1, k1, vt1, o1, m_ref, alpha_ref, acc_ref, s_ref, mode="causal", tq=tq)
    _flash_kernel(qt2, k2, vt2, o2, m_ref, alpha_ref, acc_ref, s_ref, mode="chunk", tq=tq)


def _flash_pair(fox, mla, *, tq):
    b, nh, s, w = fox[1].shape
    nq = s // tq
    assert tq % ROWS == 0
    for qt, k, vt in (fox, mla):
        assert k.shape == (b, nh, s, w) and qt.shape == vt.shape == (b, nq, nh, w, tq)
    branch_specs = [pl.BlockSpec((None, None, nh, w, tq), lambda bi, i: (bi, i, 0, 0, 0)),
                    pl.BlockSpec((None, nh, s, w), lambda bi, i: (bi, 0, 0, 0)),
                    pl.BlockSpec((None, nq, nh, w, tq), lambda bi, i: (bi, 0, 0, 0, 0))]
    out_spec = pl.BlockSpec((None, tq, nh * DEN_ROW), lambda bi, i: (bi, i, 0))
    out_shape = jax.ShapeDtypeStruct((b, s, nh * DEN_ROW), BF16)
    return pl.pallas_call(
        functools.partial(_flash_pair_kernel, tq=tq),
        grid=(b, nq),
        in_specs=branch_specs * 2,
        out_specs=[out_spec] * 2,
        out_shape=[out_shape] * 2,
        scratch_shapes=[pltpu.VMEM((nh, 1, tq), F32),
                        pltpu.VMEM((nh, 1, tq), F32),
                        pltpu.VMEM((nh, w, tq), F32),
                        pltpu.VMEM((2, tq, tq), F32)],
        compiler_params=_params(("parallel", "arbitrary")),
        name="flash",
    )(*fox, *mla)


def _merge_kernel(x_ref, of_ref, og_ref, om_ref, g_ref, wz_ref, bz_ref, wf_ref, wgl_ref, wm_ref,
                  wo_ref, o_ref):
    x = x_ref[...]
    d = x.shape[-1]
    h = _rms(x, g_ref[...]).astype(BF16)
    y = None
    for br, (b_ref, w_ref) in enumerate(((of_ref, wf_ref), (og_ref, wgl_ref), (om_ref, wm_ref))):
        cs = slice(br * d, (br + 1) * d)
        gate = jax.nn.sigmoid(_dot(h, wz_ref[:, cs]) + bz_ref[:, cs])
        term = gate * _dot(b_ref[...], w_ref[...])
        y = term if y is None else y + term
    o_ref[...] = x + _dot(y.astype(BF16), wo_ref[...])


def _merge(x2d, of, og, om, g, wz, bz, wf, wgl, wm, wo, tm):
    t, d = x2d.shape
    row = lambda w: pl.BlockSpec((tm, w), lambda i: (i, 0))
    return pl.pallas_call(
        _merge_kernel,
        grid=(t // tm,),
        in_specs=[row(d), row(of.shape[1]), row(og.shape[1]), row(om.shape[1]), _full(g.shape),
                  _full(wz.shape), _full(bz.shape), _full(wf.shape), _full(wgl.shape),
                  _full(wm.shape), _full(wo.shape)],
        out_specs=row(d),
        out_shape=jax.ShapeDtypeStruct((t, d), F32),
        compiler_params=_params(("parallel",)),
        name="merge",
    )(x2d, of, og, om, g, wz, bz, wf, wgl, wm, wo)


def _memkv_kernel(m_ref, g_ref, w_ref, k_ref, v_ref):
    h = _rms(m_ref[...], g_ref[...]).astype(BF16)
    kv = _dot(h, w_ref[...])
    w = k_ref.shape[-1]
    k_ref[...] = kv[:, :w].astype(BF16)
    v_ref[...] = kv[:, w:].T.astype(BF16)


def _memkv(mem2d, g, w, tm):
    t, d = mem2d.shape
    xw = w.shape[1] // 2
    return pl.pallas_call(
        _memkv_kernel,
        grid=(t // tm,),
        in_specs=[pl.BlockSpec((tm, d), lambda i: (i, 0)), _full(g.shape), _full(w.shape)],
        out_specs=[pl.BlockSpec((tm, xw), lambda i: (i, 0)), pl.BlockSpec((xw, tm), lambda i: (0, i))],
        out_shape=[jax.ShapeDtypeStruct((t, xw), BF16), jax.ShapeDtypeStruct((xw, t), BF16)],
        compiler_params=_params(("parallel",)),
        name="memkv",
    )(mem2d, g, w)


def _xattn_kernel(x_ref, k_ref, vt_ref, g_ref, wq_ref, wo_ref, o_ref):
    x = x_ref[...]
    h = _rms(x, g_ref[...]).astype(BF16)
    q = _dot(h, wq_ref[...]).astype(BF16)
    heads = [slice(hd * XA_HD, (hd + 1) * XA_HD) for hd in range(XA_HEADS)]
    scores = [_dot_nt(k_ref[:, sl], q[:, sl]) * (XA_HD ** -0.5) for sl in heads]
    probs = []
    for st in scores:
        e = jnp.exp(st - jnp.max(st, axis=0, keepdims=True))
        probs.append((e / jnp.sum(e, axis=0, keepdims=True)).astype(BF16))
    outs = [_dot(vt_ref[sl, :], p) for sl, p in zip(heads, probs)]
    o = jnp.concatenate(outs, axis=0).T.astype(BF16)
    o_ref[...] = x + _dot(o, wo_ref[...])


def _xattn(x, k, vt, g, wq, wo, tm):
    b, s, d = x.shape
    m, xw = k.shape[1], k.shape[2]
    return pl.pallas_call(
        _xattn_kernel,
        grid=(b, s // tm),
        in_specs=[pl.BlockSpec((None, tm, d), lambda bi, i: (bi, i, 0)),
                  pl.BlockSpec((None, m, xw), lambda bi, i: (bi, 0, 0)),
                  pl.BlockSpec((xw, m), lambda bi, i: (0, bi)),
                  _full(g.shape), _full(wq.shape), _full(wo.shape)],
        out_specs=pl.BlockSpec((None, tm, d), lambda bi, i: (bi, i, 0)),
        out_shape=jax.ShapeDtypeStruct(x.shape, F32),
        compiler_params=_params(("parallel", "parallel")),
        name="xattn",
    )(x, k, vt, g, wq, wo)


def _mlp_kernel(x_ref, g_ref, w1_ref, w2_ref, gf_ref, o_ref, *, ff_tile, final_norm):
    x = x_ref[...]
    h = _rms(x, g_ref[...]).astype(BF16)
    acc = x
    for f0 in range(0, w1_ref.shape[1], ff_tile):
        a = jnp.square(jnp.maximum(_dot(h, w1_ref[:, f0:f0 + ff_tile]), 0.0))
        acc = acc + _dot(a.astype(BF16), w2_ref[f0:f0 + ff_tile, :])
    o_ref[...] = _rms(acc, gf_ref[...]) if final_norm else acc


def _mlp(x2d, g, w1, w2, gf, tm, final_norm):
    t, d = x2d.shape
    row = pl.BlockSpec((tm, d), lambda i: (i, 0))
    return pl.pallas_call(
        functools.partial(_mlp_kernel, ff_tile=min(1024, w1.shape[1]), final_norm=final_norm),
        grid=(t // tm,),
        in_specs=[row, _full(g.shape), _full(w1.shape), _full(w2.shape), _full(gf.shape)],
        out_specs=row,
        out_shape=jax.ShapeDtypeStruct((t, d), F32),
        compiler_params=_params(("parallel",)),
        name="mlp",
    )(x2d, g, w1, w2, gf)


def _slot_cols(w, heads, hd):
    k = w.shape[0]
    w = w.reshape(k, heads, hd)
    return jnp.pad(w, ((0, 0), (0, 0), (0, LANES - hd))).reshape(k, heads * LANES)


REPACK_PIECES = 3


def _repack_kernel(blk_ref, par_ref, a_ref, b_ref, o_ref, *, n_cols):
    i = pl.program_id(0)
    n_layers, k = a_ref.shape[1], a_ref.shape[2]
    row = blk_ref[i] * LANES + lax.broadcasted_iota(jnp.int32, (2 * LANES, k), 0)
    j = lax.broadcasted_iota(jnp.int32, (LANES, 2 * LANES), 0)
    r = lax.broadcasted_iota(jnp.int32, (LANES, 2 * LANES), 1)
    sel = None
    for p in range(REPACK_PIECES):
        off = par_ref[(i * REPACK_PIECES + p) * 3]
        width = par_ref[(i * REPACK_PIECES + p) * 3 + 1]
        dst = par_ref[(i * REPACK_PIECES + p) * 3 + 2]
        hit = (r - off == j - dst) & (j >= dst) & (j < dst + width)
        sel = hit if sel is None else sel | hit
    sel = sel.astype(BF16)
    for l in range(n_layers):
        window = jnp.concatenate([a_ref[:, l, :], b_ref[:, l, :]], axis=0)
        window = jnp.where(row < n_cols, window, 0.0).astype(BF16)
        o_ref[l] = _dot(sel, window).T.astype(BF16)


def _repack(w_all, blocks):
    n_layers, k, n_cols = w_all.shape
    last_blk = (n_cols - 1) // LANES
    blk, par = [], []
    for pieces in blocks:
        b0 = min(p[0] for p in pieces) // LANES
        assert len(pieces) <= REPACK_PIECES
        pieces = list(pieces) + [(b0 * LANES, 0, 0)] * (REPACK_PIECES - len(pieces))
        for src, width, dst in pieces:
            assert 0 <= src - b0 * LANES and src - b0 * LANES + width <= 2 * LANES and dst + width <= LANES
            par += [src - b0 * LANES, width, dst]
        blk.append(b0)
    w_t = jnp.transpose(w_all, (2, 0, 1))
    grid_spec = pltpu.PrefetchScalarGridSpec(
        num_scalar_prefetch=2,
        grid=(len(blocks),),
        in_specs=[pl.BlockSpec((LANES, n_layers, k), lambda i, blk, par: (blk[i], 0, 0)),
                  pl.BlockSpec((LANES, n_layers, k),
                               lambda i, blk, par: (jnp.minimum(blk[i] + 1, last_blk), 0, 0))],
        out_specs=pl.BlockSpec((n_layers, k, LANES), lambda i, blk, par: (0, 0, i)),
    )
    return pl.pallas_call(
        functools.partial(_repack_kernel, n_cols=n_cols),
        grid_spec=grid_spec,
        out_shape=jax.ShapeDtypeStruct((n_layers, k, len(blocks) * LANES), BF16),
        compiler_params=_params(("arbitrary",)),
        name="repack",
    )(jnp.asarray(np.array(blk, np.int32)), jnp.asarray(np.array(par, np.int32)), w_t, w_t)


def _inproj_weights(w_all):
    d = w_all.shape[1]
    fw, kw, vw = FOX_HEADS * FOX_HD, GLA_HEADS * GLA_DK, GLA_HEADS * GLA_DV
    sizes = (fw, fw, fw, FOX_HEADS, kw, kw, vw, GLA_GATE_RANK, vw,
             MLA_Q_RANK, MLA_KV_RANK, MLA_ROPE, N_BRANCH * d)
    starts = np.concatenate([[0], np.cumsum(sizes)]).tolist()
    fq, fk, fv, ff, gq, gk, gv, glow, gr, mq, mkv, mkr, zg = starts[:-1]
    half = MLA_ROPE // 2
    dense = lambda c0, width: [[(c0 + i, LANES, 0)] for i in range(0, width, LANES)]
    blocks = (dense(fq, fw) + dense(fk, fw) + dense(fv, fw)
              + dense(gq, kw) + dense(gk, kw) + dense(gv, vw) + dense(gr, vw)
              + [[(ff, FOX_HEADS, p * FOX_HEADS) for p in range(N_PIECES)],
                 [(glow, GLA_GATE_RANK, 0)], [(mkr, MLA_ROPE, ROPE_LANE)],
                 [(mkr + half, half, ROPE_LANE), (mkr, half, ROPE_LANE + half)]]
              + dense(mq, MLA_Q_RANK) + dense(mkv, MLA_KV_RANK))
    n, tr = False, True
    outs = [(fw, BF16, n), (fw, BF16, n), (fw, BF16, n), (kw, BF16, n), (kw, F32, n),
            (vw, BF16, tr), (vw, F32, n), (LANES, F32, n), (LANES, F32, n), (LANES, F32, n),
            (LANES, F32, n), (MLA_Q_RANK, F32, n), (MLA_KV_RANK, F32, n)]
    return _repack(w_all, blocks), outs, _repack(w_all, dense(zg, N_BRANCH * d))


def _mla_weights(w_uq, w_ukv):
    half = MLA_ROPE // 2
    qk = MLA_NOPE + MLA_ROPE
    r = w_uq.shape[0]
    wq = w_uq.reshape(r, MLA_HEADS, qk)
    nope, x1, x2 = wq[..., :MLA_NOPE], wq[..., MLA_NOPE:MLA_NOPE + half], wq[..., MLA_NOPE + half:]
    pad = jnp.zeros((r, MLA_HEADS, LANES - qk), F32)
    wqa = jnp.concatenate([nope, x1, x2, pad], axis=-1).reshape(r, SLOTS)
    wqb = jnp.concatenate([jnp.zeros_like(nope), x2, x1, pad], axis=-1).reshape(r, SLOTS)
    rk = w_ukv.shape[0]
    wkv = w_ukv.reshape(rk, MLA_HEADS, MLA_NOPE + MLA_VD)
    wk = _slot_cols(wkv[..., :MLA_NOPE].reshape(rk, -1), MLA_HEADS, MLA_NOPE)
    wv = _slot_cols(wkv[..., MLA_NOPE:].reshape(rk, -1), MLA_HEADS, MLA_VD)
    return wqa.astype(BF16), wqb.astype(BF16), wk.astype(BF16), wv.astype(BF16)


def _rope_tables(seq):
    half = MLA_ROPE // 2
    inv = ROPE_BASE ** (-jnp.arange(half, dtype=F32) / half)
    ang = jnp.arange(seq).astype(F32)[:, None] * inv[None, :]
    cos, sin = jnp.cos(ang), jnp.sin(ang)
    one = jnp.ones((seq, MLA_NOPE), F32)
    zero = jnp.zeros((seq, LANES - MLA_NOPE - MLA_ROPE), F32)
    ct = jnp.concatenate([one, cos, cos, zero], axis=1)
    st = jnp.concatenate([jnp.zeros_like(one), -sin, sin, zero], axis=1)
    return ct, st


def _tile(n, pref):
    t = min(n, pref)
    assert n % t == 0, (n, t)
    return t


def kernel(x, mem, g_mix, w_in, b_fox_forget, w_gla_gate, b_gla_gate, g_gla_out, g_mla_q, w_mla_uq, g_mla_kv, w_mla_ukv, b_branch_gate, w_up_fox, w_up_gla, w_up_mla, w_out, g_xa, g_mem, w_xq, w_xkv, w_xo, g_mlp, w_mlp1, w_mlp2, g_final):
    b, s, d = x.shape
    depth = w_in.shape[0]
    t = b * s
    mlen = mem.shape[1]
    tm = _tile(t, 1024)
    ts = _tile(s, 512)
    tq = _tile(s, 512)
    ct, st = _rope_tables(s)
    x2d = x.reshape(t, d)
    mem2d = mem.reshape(b * mlen, d)
    row = lambda v: v.reshape(1, -1)
    w1_all, outs, wz_all = _inproj_weights(w_in)
    for l in range(depth):
        w1, wz = w1_all[l], wz_all[l]
        fq, fk, fv, gq, gk, gv, gr, ff, glow, kr, krs, cq, ckv = _inproj(x2d, row(g_mix[l]), w1, outs, tm)
        as3 = lambda a: a.reshape(b, s, a.shape[-1])
        bf = jnp.pad(jnp.tile(b_fox_forget[l], N_PIECES), (0, LANES - N_PIECES * FOX_HEADS)).reshape(1, LANES)
        wqa, wqb, wk, wv = _mla_weights(w_mla_uq[l], w_mla_ukv[l])
        fox, mla = _prep(as3(fq), as3(fk), as3(fv), as3(ff), bf, cq, ckv, kr, krs, row(g_mla_q[l]),
                         row(g_mla_kv[l]), wqa, wqb, wk, wv, ct, st, ts)
        o_fox, o_mla = _flash_pair(fox, mla, tq=tq)
        wg = jnp.pad(w_gla_gate[l], ((0, LANES - GLA_GATE_RANK), (0, 0))).astype(BF16)
        o_gla = _gla(as3(gq), as3(gk), gv, as3(gr), as3(glow), wg, row(b_gla_gate[l]),
                     row(g_gla_out[l]), _tile(s, 1024))
        x2d = _merge(x2d, o_fox.reshape(t, -1), o_gla.reshape(t, -1), o_mla.reshape(t, -1),
                     row(g_mix[l]), wz, row(b_branch_gate[l]), w_up_fox[l].astype(BF16),
                     w_up_gla[l].astype(BF16), w_up_mla[l].astype(BF16), w_out[l].astype(BF16), tm)
        km, vmt = _memkv(mem2d, row(g_mem[l]), w_xkv[l].astype(BF16), _tile(b * mlen, 512))
        x2d = _xattn(x2d.reshape(b, s, d), km.reshape(b, mlen, -1), vmt,
                     row(g_xa[l]), w_xq[l].astype(BF16), w_xo[l].astype(BF16),
                     _tile(s, 1024)).reshape(t, d)
        x2d = _mlp(x2d, row(g_mlp[l]), w_mlp1[l].astype(BF16), w_mlp2[l].astype(BF16),
                   row(g_final), tm, final_norm=(l == depth - 1))
    return x2d.reshape(b, s, d)
```

```python
import functools

import numpy as np
import jax
import jax.numpy as jnp
from jax import lax
from jax.experimental import pallas as pl
from jax.experimental.pallas import tpu as pltpu

F32 = jnp.float32
BF16 = jnp.bfloat16

CHUNK = 64
EPS = 1e-6
FOX_HEADS, FOX_HD = 4, 64
GLA_HEADS, GLA_DK, GLA_DV, GLA_GATE_RANK, GLA_TAU = 4, 64, 128, 16, 16.0
MLA_HEADS, MLA_Q_RANK, MLA_KV_RANK, MLA_NOPE, MLA_ROPE, MLA_VD = 4, 256, 128, 64, 32, 64
ROPE_BASE = 10000.0
XA_HEADS, XA_HD = 4, 128
N_BRANCH = 3

LANES = 128
MXU_DEPTH = 256
VMEM_LIMIT = 56 * 1024 * 1024

N_HEADS = 4
SLOTS = N_HEADS * LANES
MASK_VALUE = -1e30

DEC_LANE = FOX_HD
ROPE_LANE = MLA_NOPE
DEN_ROW = 64
assert DEN_ROW == FOX_HD == MLA_VD
LOG2E = 1.4426950408889634
MLA_SCALE = (MLA_NOPE + MLA_ROPE) ** -0.5


def _params(sem):
    return pltpu.CompilerParams(dimension_semantics=sem, vmem_limit_bytes=VMEM_LIMIT)


def _dot(a, b):
    return jnp.dot(a, b, preferred_element_type=F32)


def _dot_nt(a, b):
    return lax.dot_general(a, b, (((1,), (1,)), ((), ())), preferred_element_type=F32)


def _dot_tn(a, b):
    return lax.dot_general(a, b, (((0,), (0,)), ((), ())), preferred_element_type=F32)


def _rms(x, g):
    y = x * lax.rsqrt(jnp.mean(x * x, axis=-1, keepdims=True) + EPS)
    return y * g


def _log_sigmoid(x):
    return -(jnp.maximum(-x, 0.0) + jnp.log1p(jnp.exp(-jnp.abs(x))))


N_PIECES = 3


def _split3(x):
    p1 = x.astype(BF16)
    r1 = x - p1.astype(F32)
    p2 = r1.astype(BF16)
    r2 = r1 - p2.astype(F32)
    return p1, p2, r2.astype(BF16)


def _lower_tri(n):
    r = lax.broadcasted_iota(jnp.int32, (n, n), 0)
    c = lax.broadcasted_iota(jnp.int32, (n, n), 1)
    return (r >= c).astype(BF16)


def _cumsum_rows(tri, pieces):
    return _dot(tri, pieces[0]) + _dot(tri, pieces[1]) + _dot(tri, pieces[2])


def _div_pow2(x, n):
    assert n & (n - 1) == 0, n
    return jnp.right_shift(x, n.bit_length() - 1)


def _full(shape):
    return pl.BlockSpec(shape, lambda *_: (0,) * len(shape), pipeline_mode=pl.Buffered(1))


INPROJ_CHUNK = 2 * MXU_DEPTH


def _inproj_kernel(x_ref, g_ref, w_ref, *out_refs, cols):
    h = _rms(x_ref[...], g_ref[...]).astype(BF16)
    n = w_ref.shape[1]
    for c0 in range(0, n, INPROJ_CHUNK):
        c1 = min(c0 + INPROJ_CHUNK, n)
        y = _dot(h, w_ref[:, c0:c1])
        for o_ref, (o0, ow, transposed) in zip(out_refs, cols):
            lo, hi = max(o0, c0), min(o0 + ow, c1)
            if lo >= hi:
                continue
            piece = y[:, lo - c0:hi - c0]
            if transposed:
                o_ref[lo - o0:hi - o0, :] = piece.T.astype(o_ref.dtype)
            else:
                o_ref[:, lo - o0:hi - o0] = piece.astype(o_ref.dtype)


def _inproj(x2d, g, w, outs, tm):
    t, d = x2d.shape
    cols, c0 = [], 0
    for cw, _, tr in outs:
        cols.append((c0, cw, tr))
        c0 += cw
    return pl.pallas_call(
        functools.partial(_inproj_kernel, cols=tuple(cols)),
        grid=(t // tm,),
        in_specs=[pl.BlockSpec((tm, d), lambda i: (i, 0)), _full((1, d)), _full(w.shape)],
        out_specs=[pl.BlockSpec((cw, tm), lambda i: (0, i)) if tr else pl.BlockSpec((tm, cw), lambda i: (i, 0))
                   for cw, _, tr in outs],
        out_shape=[jax.ShapeDtypeStruct((cw, t) if tr else (t, cw), dt) for cw, dt, tr in outs],
        compiler_params=_params(("parallel",)),
        name="inproj",
    )(x2d, g, w)


def _fox_prep_kernel(fq_ref, fk_ref, fv_ref, ff_ref, bf_ref, eqt_ref, ek_ref, es_ref, qo_ref, ko_ref,
                     vo_ref, carry_ref, *, ts):
    @pl.when(pl.program_id(1) == 0)
    def _():
        carry_ref[...] = jnp.zeros_like(carry_ref)

    lane = lax.broadcasted_iota(jnp.int32, (ts, LANES), 1)
    live = lane < N_PIECES * FOX_HEADS
    z = jnp.where(live, ff_ref[...] + bf_ref[...], 0.0)
    lf = jnp.where(live, _log_sigmoid(z), 0.0)
    cum3 = _dot(_lower_tri(ts), jnp.concatenate(_split3(lf), axis=1))
    cum = cum3[:, :LANES] + cum3[:, LANES:2 * LANES] + cum3[:, 2 * LANES:] + carry_ref[0:1, :]
    carry_ref[...] = jnp.broadcast_to(cum[ts - 1:ts, :], carry_ref.shape)
    p1, p2, p3 = _split3(cum * LOG2E)
    one = jnp.where(lane == N_PIECES * FOX_HEADS, 1.0, 0.0).astype(BF16)
    pieces = jnp.where(lane < FOX_HEADS, p1, jnp.where(lane < 2 * FOX_HEADS, p2,
                       jnp.where(lane < 3 * FOX_HEADS, p3, one)))
    q_t = fq_ref[...].astype(F32).T * (FOX_HD ** -0.5 * LOG2E)
    v_t = fv_ref[...].astype(F32).T
    aug_qt = _dot(eqt_ref[...], pieces.astype(F32).T.astype(BF16))
    k_slots = _dot(fk_ref[...], es_ref[...]) + _dot(pieces, ek_ref[...])
    den_rows = jnp.where(lax.broadcasted_iota(jnp.int32, (LANES - FOX_HD, ts), 0) == 0, 1.0, 0.0)
    for h in range(FOX_HEADS):
        ch = slice(h * FOX_HD, (h + 1) * FOX_HD)
        spare = slice(h * LANES + FOX_HD, (h + 1) * LANES)
        qo_ref[h] = jnp.concatenate([q_t[ch], aug_qt[spare]], axis=0).astype(BF16)
        ko_ref[h] = k_slots[:, h * LANES:(h + 1) * LANES].astype(BF16)
        vo_ref[h] = jnp.concatenate([v_t[ch], den_rows], axis=0).astype(BF16)


def _fox_placement():
    eq = np.zeros((LANES, SLOTS), np.float32)
    ek = np.zeros((LANES, SLOTS), np.float32)
    one_lane = N_PIECES * FOX_HEADS
    for h in range(FOX_HEADS):
        base = h * LANES + DEC_LANE
        for p in range(N_PIECES):
            eq[p * FOX_HEADS + h, base + p] = 1.0
            eq[one_lane, base + N_PIECES + p] = 1.0
            ek[one_lane, base + p] = 1.0
            ek[p * FOX_HEADS + h, base + N_PIECES + p] = -1.0
    es = np.zeros((FOX_HEADS * FOX_HD, SLOTS), np.float32)
    for h in range(FOX_HEADS):
        for c in range(FOX_HD):
            es[h * FOX_HD + c, h * LANES + c] = 1.0
    return jnp.asarray(eq.T, BF16), jnp.asarray(ek, BF16), jnp.asarray(es, BF16)


def _head_major_specs(batch, seq, ts):
    spec_t = pl.BlockSpec((None, None, N_HEADS, LANES, ts), lambda bi, i: (bi, i, 0, 0, 0))
    spec_n = pl.BlockSpec((None, N_HEADS, ts, LANES), lambda bi, i: (bi, 0, i, 0))
    shape_t = jax.ShapeDtypeStruct((batch, seq // ts, N_HEADS, LANES, ts), BF16)
    shape_n = jax.ShapeDtypeStruct((batch, N_HEADS, seq, LANES), BF16)
    return spec_t, spec_n, shape_t, shape_n


N_FOX_IN, N_MLA_IN, N_PREP_OUT = 8, 12, 3


def _prep_kernel(*refs, ts):
    fox_in = refs[:N_FOX_IN]
    mla_in = refs[N_FOX_IN:N_FOX_IN + N_MLA_IN]
    outs = refs[N_FOX_IN + N_MLA_IN:N_FOX_IN + N_MLA_IN + 2 * N_PREP_OUT]
    carry_ref = refs[-1]
    _fox_prep_kernel(*fox_in, *outs[:N_PREP_OUT], carry_ref, ts=ts)
    _mla_prep_kernel(*mla_in, *outs[N_PREP_OUT:], ts=ts)


def _prep(fq, fk, fv, ff, bf, cq, ckv, kr, krs, gq, gkv, wqa, wqb, wk, wv, ct, st, ts):
    b, s, w = fq.shape
    nblk = s // ts
    spec = pl.BlockSpec((None, ts, w), lambda bi, i: (bi, i, 0))
    row = lambda width: pl.BlockSpec((ts, width), lambda bi, i: (bi * nblk + i, 0))
    tab = pl.BlockSpec((ts, LANES), lambda bi, i: (i, 0))
    spec_t, spec_n, shape_t, shape_n = _head_major_specs(b, s, ts)
    eq, ek, es = _fox_placement()
    fox_specs = [spec, spec, spec, pl.BlockSpec((None, ts, LANES), lambda bi, i: (bi, i, 0)),
                 _full((1, LANES)), _full(eq.shape), _full(ek.shape), _full(es.shape)]
    mla_specs = [row(MLA_Q_RANK), row(MLA_KV_RANK), row(LANES), row(LANES), _full(gq.shape),
                 _full(gkv.shape), _full(wqa.shape), _full(wqb.shape), _full(wk.shape),
                 _full(wv.shape), tab, tab]
    assert len(fox_specs) == N_FOX_IN and len(mla_specs) == N_MLA_IN
    outs = pl.pallas_call(
        functools.partial(_prep_kernel, ts=ts),
        grid=(b, nblk),
        in_specs=fox_specs + mla_specs,
        out_specs=[spec_t, spec_n, spec_t] * 2,
        out_shape=[shape_t, shape_n, shape_t] * 2,
        scratch_shapes=[pltpu.VMEM((8, LANES), F32)],
        compiler_params=_params(("parallel", "arbitrary")),
        name="prep",
    )(fq, fk, fv, ff, bf, eq, ek, es, cq, ckv, kr, krs, gq, gkv, wqa, wqb, wk, wv, ct, st)
    return outs[:N_PREP_OUT], outs[N_PREP_OUT:]


def _mla_prep_kernel(cq_ref, ckv_ref, kr_ref, krs_ref, gq_ref, gkv_ref, wqa_ref, wqb_ref,
                     wk_ref, wv_ref, ct_ref, st_ref, q_out, k_out, v_out, *, ts):
    hq = _rms(cq_ref[...], gq_ref[...]).astype(BF16)
    hkv = _rms(ckv_ref[...], gkv_ref[...]).astype(BF16)
    ct = ct_ref[...]
    st = st_ref[...]
    lane = lax.broadcasted_iota(jnp.int32, (ts, LANES), 1)
    in_rope = (lane >= ROPE_LANE) & (lane < ROPE_LANE + MLA_ROPE)
    k_rope = jnp.where(in_rope, kr_ref[...] * ct + krs_ref[...] * st, 0.0)
    qa = _dot(hq, wqa_ref[...])
    qb = _dot(hq, wqb_ref[...])
    kk = _dot(hkv, wk_ref[...])
    vv = _dot(hkv, wv_ref[...])
    for h in range(MLA_HEADS):
        sl = slice(h * LANES, (h + 1) * LANES)
        q_out[h] = ((qa[:, sl] * ct + qb[:, sl] * st) * (MLA_SCALE * LOG2E)).T.astype(BF16)
        k_out[h] = (kk[:, sl] + k_rope).astype(BF16)
        v_out[h] = jnp.where(lane == DEN_ROW, 1.0, vv[:, sl]).T.astype(BF16)


def _gla_kernel(gq_ref, gk_ref, gvt_ref, gr_ref, glow_ref, wg_ref, bg_ref, go_ref, o_ref,
                state_ref, *, tc):
    kw = GLA_HEADS * GLA_DK
    vw = GLA_HEADS * GLA_DV
    assert 2 * GLA_DK == LANES and GLA_DV == LANES

    @pl.when(pl.program_id(1) == 0)
    def _():
        state_ref[...] = jnp.zeros_like(state_ref)

    first_of_pair = lax.broadcasted_iota(jnp.int32, (GLA_DV, LANES), 1) < GLA_DK
    q_first = lax.broadcasted_iota(jnp.int32, (CHUNK, LANES), 1) < GLA_DK
    gr_rows = MXU_DEPTH
    ri = lax.broadcasted_iota(jnp.int32, (gr_rows, gr_rows), 0)
    ci = lax.broadcasted_iota(jnp.int32, (gr_rows, gr_rows), 1)
    tri = ((_div_pow2(ri, CHUNK) == _div_pow2(ci, CHUNK)) & (ri >= ci)).astype(BF16)
    chunk_of_row = _div_pow2(lax.broadcasted_iota(jnp.int32, (gr_rows, kw), 0), CHUNK)
    g_out = go_ref[...]
    pairs = range(GLA_HEADS // 2)
    chunks_per_group = gr_rows // CHUNK
    ends, k_decs, qs = [], [], []
    for g0 in range(0, tc, gr_rows):
        gs = slice(g0, g0 + gr_rows)
        gate = _dot(glow_ref[gs, :].astype(BF16), wg_ref[...]) + bg_ref[...]
        cum = _cumsum_rows(tri, _split3(_log_sigmoid(gate) * (1.0 / GLA_TAU)))
        g_ends = [cum[(c + 1) * CHUNK - 1:(c + 1) * CHUNK, :] for c in range(chunks_per_group)]
        end_rows = jnp.concatenate([jnp.broadcast_to(e, (CHUNK, kw)) for e in g_ends], axis=0)
        ends += g_ends
        k_decs.append(gk_ref[gs, :] * jnp.exp(end_rows - cum))
        qs.append((gq_ref[gs, :].astype(F32) * (GLA_DK ** -0.5)).astype(BF16))
    u_pairs = []
    for n in range(tc // CHUNK):
        g, c = divmod(n, chunks_per_group)
        k_c = jnp.where(chunk_of_row == c, k_decs[g], 0.0).astype(BF16)
        u_t = _dot(gvt_ref[:, g * gr_rows:(g + 1) * gr_rows], k_c)
        u_pairs.append([jnp.where(first_of_pair,
                                  u_t[2 * p * GLA_DV:(2 * p + 1) * GLA_DV, p * LANES:(p + 1) * LANES],
                                  u_t[(2 * p + 1) * GLA_DV:(2 * p + 2) * GLA_DV, p * LANES:(p + 1) * LANES])
                        for p in pairs])
    states = [state_ref[p] for p in pairs]
    chunk_states = []
    for n in range(tc // CHUNK):
        a = jnp.exp(ends[n])
        states = [a[:, p * LANES:(p + 1) * LANES] * states[p] + u_pairs[n][p] for p in pairs]
        chunk_states.append([st.astype(BF16) for st in states])
    for p in pairs:
        state_ref[p] = states[p]
    for n in range(tc // CHUNK):
        g, c = divmod(n, chunks_per_group)
        rows = slice(n * CHUNK, (n + 1) * CHUNK)
        for p in pairs:
            q_pair = qs[g][c * CHUNK:(c + 1) * CHUNK, p * LANES:(p + 1) * LANES]
            for half in range(2):
                h = 2 * p + half
                q_h = jnp.where(q_first if half == 0 else jnp.logical_not(q_first), q_pair, 0.0)
                o = _dot_nt(q_h.astype(BF16), chunk_states[n][p])
                sl = slice(h * GLA_DV, (h + 1) * GLA_DV)
                r = gr_ref[rows, sl]
                o_ref[rows, sl] = (_rms(o, g_out) * (r * jax.nn.sigmoid(r))).astype(BF16)


def _gla(gq, gk, gvt, gr, glow, wg, bg, go, tc):
    b, s, _ = gq.shape
    nblk = s // tc
    spec = lambda w: pl.BlockSpec((None, tc, w), lambda bi, i: (bi, i, 0))
    kw, vw = GLA_HEADS * GLA_DK, GLA_HEADS * GLA_DV
    return pl.pallas_call(
        functools.partial(_gla_kernel, tc=tc),
        grid=(b, nblk),
        in_specs=[spec(kw), spec(kw), pl.BlockSpec((vw, tc), lambda bi, i: (0, bi * nblk + i)),
                  spec(vw), spec(LANES), _full(wg.shape), _full(bg.shape), _full(go.shape)],
        out_specs=spec(vw),
        out_shape=jax.ShapeDtypeStruct((b, s, vw), BF16),
        scratch_shapes=[pltpu.VMEM((GLA_HEADS // 2, GLA_DV, LANES), F32)],
        compiler_params=_params(("parallel", "arbitrary")),
        name="gla",
    )(gq, gk, gvt, gr, glow, wg, bg, go)


ROWS = MXU_DEPTH
V_ROWS = 80
assert V_ROWS > DEN_ROW and V_ROWS % 16 == 0


def _flash_kernel(qt_ref, k_ref, vt_ref, o_ref, m_ref, alpha_ref, acc_ref, s_ref, *, mode, tq):
    i = pl.program_id(1)
    sub = tq
    assert sub % CHUNK == 0
    m_ref[...] = jnp.full_like(m_ref, MASK_VALUE)
    acc_ref[...] = jnp.zeros_like(acc_ref)

    def score_chunk(base, item, buf, r, m_run):
        u, h, masked = item
        k0 = pl.multiple_of((base + u) * sub + r, ROWS)
        q0 = r if masked else 0
        s = _dot(k_ref[h, pl.ds(k0, ROWS), :], qt_ref[h, :, q0:])
        if masked:
            s_pos = k0 + lax.broadcasted_iota(jnp.int32, (ROWS, tq - q0), 0)
            t_pos = i * tq + q0 + lax.broadcasted_iota(jnp.int32, (ROWS, tq - q0), 1)
            if mode == "causal":
                keep = s_pos <= t_pos
            else:
                keep = _div_pow2(s_pos, CHUNK) <= _div_pow2(t_pos, CHUNK)
            s = jnp.where(keep, s, MASK_VALUE)
        s_ref[buf, r:r + ROWS, q0:] = s
        m_new = jnp.maximum(m_run[:, q0:], jnp.max(s, axis=0, keepdims=True))
        return m_new if q0 == 0 else jnp.concatenate([m_run[:, :q0], m_new], axis=1)

    def value_chunk(base, item, buf, r, m_new):
        u, h, masked = item
        q0 = r if masked else 0
        p = jnp.exp2(s_ref[buf, r:r + ROWS, q0:] - m_new[:, q0:]).astype(BF16)
        d = _dot(vt_ref[base + u, h, 0:V_ROWS, r:r + ROWS], p)
        return d if q0 == 0 else jnp.concatenate([jnp.zeros((V_ROWS, q0), F32), d], axis=1)

    def finish_scores(h, m_prev, m_new):
        alpha_ref[h] = jnp.exp2(m_prev - m_new)
        m_ref[h] = m_new

    def body(base, sub_masked):
        items = [(u, h, m) for u, m in enumerate(sub_masked) for h in range(N_HEADS)]
        chunks = range(0, sub, ROWS)
        h0 = items[0][1]
        m_prev = m_ref[h0]
        m_run = m_prev
        for r in chunks:
            m_run = score_chunk(base, items[0], 0, r, m_run)
        finish_scores(h0, m_prev, m_run)
        for n, item in enumerate(items):
            h = item[1]
            nxt = items[n + 1] if n + 1 < len(items) else None
            m_new = m_ref[h]
            if nxt is not None:
                m_prev = m_ref[nxt[1]]
                m_run = m_prev
            pv = None
            for r in chunks:
                if nxt is not None:
                    m_run = score_chunk(base, nxt, (n + 1) % 2, r, m_run)
                d = value_chunk(base, item, n % 2, r, m_new)
                pv = d if pv is None else pv + d
            if nxt is not None:
                finish_scores(nxt[1], m_prev, m_run)
            acc_ref[h, 0:V_ROWS, :] = alpha_ref[h] * acc_ref[h, 0:V_ROWS, :] + pv

    def full_pair(t, carry):
        body(2 * t, (False, False))
        return carry

    lax.fori_loop(0, jnp.right_shift(i, 1), full_pair, 0)

    @pl.when(jnp.bitwise_and(i, 1) == 1)
    def _():
        body(i - 1, (False, True))

    @pl.when(jnp.bitwise_and(i, 1) == 0)
    def _():
        body(i, (True,))

    per = LANES // DEN_ROW
    for g in range(N_HEADS // per):
        rows = [acc_ref[h, 0:DEN_ROW, :] / acc_ref[h, DEN_ROW:DEN_ROW + 1, :]
                for h in range(g * per, (g + 1) * per)]
        o_ref[:, g * LANES:(g + 1) * LANES] = jnp.concatenate(rows, axis=0).T.astype(o_ref.dtype)


def _flash(qt, k, vt, *, mode, tq):
    b, nh, s, w = k.shape
    nq = s // tq
    assert tq % ROWS == 0 and qt.shape == vt.shape == (b, nq, nh, w, tq)
    return pl.pallas_call(
        functools.partial(_flash_kernel, mode=mode, tq=tq),
        grid=(b, nq),
        in_specs=[pl.BlockSpec((None, None, nh, w, tq), lambda bi, i: (bi, i, 0, 0, 0)),
                  pl.BlockSpec((None, nh, s, w), lambda bi, i: (bi, 0, 0, 0)),
                  pl.BlockSpec((None, nq, nh, w, tq), lambda bi, i: (bi, 0, 0, 0, 0))],
        out_specs=pl.BlockSpec((None, tq, nh * DEN_ROW), lambda bi, i: (bi, i, 0)),
        out_shape=jax.ShapeDtypeStruct((b, s, nh * DEN_ROW), BF16),
        scratch_shapes=[pltpu.VMEM((nh, 1, tq), F32),
                        pltpu.VMEM((nh, 1, tq), F32),
                        pltpu.VMEM((nh, w, tq), F32),
                        pltpu.VMEM((2, tq, tq), F32)],
        compiler_params=_params(("parallel", "arbitrary")),
        name="flash_" + mode,
    )(qt, k, vt)


def _merge_kernel(x_ref, of_ref, og_ref, om_ref, g_ref, wz_ref, bz_ref, wf_ref, wgl_ref, wm_ref,
                  wo_ref, o_ref):
    x = x_ref[...]
    d = x.shape[-1]
    h = _rms(x, g_ref[...]).astype(BF16)
    branches = ((of_ref, wf_ref), (og_ref, wgl_ref), (om_ref, wm_ref))
    ups = [_dot(b_ref[...], w_ref[...]) for b_ref, w_ref in branches]
    logits = [_dot(h, wz_ref[:, br * d:(br + 1) * d]) + bz_ref[:, br * d:(br + 1) * d]
              for br in range(len(branches))]
    y = None
    for up, z in zip(ups, logits):
        term = jax.nn.sigmoid(z) * up
        y = term if y is None else y + term
    o_ref[...] = x + _dot(y.astype(BF16), wo_ref[...])


def _merge(x2d, of, og, om, g, wz, bz, wf, wgl, wm, wo, tm):
    t, d = x2d.shape
    row = lambda w: pl.BlockSpec((tm, w), lambda i: (i, 0))
    return pl.pallas_call(
        _merge_kernel,
        grid=(t // tm,),
        in_specs=[row(d), row(of.shape[1]), row(og.shape[1]), row(om.shape[1]), _full(g.shape),
                  _full(wz.shape), _full(bz.shape), _full(wf.shape), _full(wgl.shape),
                  _full(wm.shape), _full(wo.shape)],
        out_specs=row(d),
        out_shape=jax.ShapeDtypeStruct((t, d), F32),
        compiler_params=_params(("parallel",)),
        name="merge",
    )(x2d, of, og, om, g, wz, bz, wf, wgl, wm, wo)


def _memkv_kernel(m_ref, g_ref, w_ref, k_ref, v_ref):
    h = _rms(m_ref[...], g_ref[...]).astype(BF16)
    kv = _dot(h, w_ref[...])
    w = k_ref.shape[-1]
    k_ref[...] = kv[:, :w].astype(BF16)
    v_ref[...] = kv[:, w:].T.astype(BF16)


def _memkv(mem2d, g, w, tm):
    t, d = mem2d.shape
    xw = w.shape[1] // 2
    return pl.pallas_call(
        _memkv_kernel,
        grid=(t // tm,),
        in_specs=[pl.BlockSpec((tm, d), lambda i: (i, 0)), _full(g.shape), _full(w.shape)],
        out_specs=[pl.BlockSpec((tm, xw), lambda i: (i, 0)), pl.BlockSpec((xw, tm), lambda i: (0, i))],
        out_shape=[jax.ShapeDtypeStruct((t, xw), BF16), jax.ShapeDtypeStruct((xw, t), BF16)],
        compiler_params=_params(("parallel",)),
        name="memkv",
    )(mem2d, g, w)


def _xattn_kernel(x_ref, k_ref, vt_ref, g_ref, wq_ref, wo_ref, o_ref):
    x = x_ref[...]
    h = _rms(x, g_ref[...]).astype(BF16)
    q = _dot(h, wq_ref[...]).astype(BF16)
    heads = [slice(hd * XA_HD, (hd + 1) * XA_HD) for hd in range(XA_HEADS)]
    scores = [_dot_nt(k_ref[:, sl], q[:, sl]) * (XA_HD ** -0.5) for sl in heads]
    probs = []
    for st in scores:
        e = jnp.exp(st - jnp.max(st, axis=0, keepdims=True))
        probs.append((e / jnp.sum(e, axis=0, keepdims=True)).astype(BF16))
    outs = [_dot(vt_ref[sl, :], p) for sl, p in zip(heads, probs)]
    o = jnp.concatenate(outs, axis=0).T.astype(BF16)
    o_ref[...] = x + _dot(o, wo_ref[...])


def _xattn(x, k, vt, g, wq, wo, tm):
    b, s, d = x.shape
    m, xw = k.shape[1], k.shape[2]
    return pl.pallas_call(
        _xattn_kernel,
        grid=(b, s // tm),
        in_specs=[pl.BlockSpec((None, tm, d), lambda bi, i: (bi, i, 0)),
                  pl.BlockSpec((None, m, xw), lambda bi, i: (bi, 0, 0)),
                  pl.BlockSpec((xw, m), lambda bi, i: (0, bi)),
                  _full(g.shape), _full(wq.shape), _full(wo.shape)],
        out_specs=pl.BlockSpec((None, tm, d), lambda bi, i: (bi, i, 0)),
        out_shape=jax.ShapeDtypeStruct(x.shape, F32),
        compiler_params=_params(("parallel", "parallel")),
        name="xattn",
    )(x, k, vt, g, wq, wo)


def _mlp_kernel(x_ref, g_ref, w1_ref, w2_ref, gf_ref, o_ref, *, ff_tile, final_norm):
    x = x_ref[...]
    h = _rms(x, g_ref[...]).astype(BF16)
    acc = x
    for f0 in range(0, w1_ref.shape[1], ff_tile):
        a = jnp.square(jnp.maximum(_dot(h, w1_ref[:, f0:f0 + ff_tile]), 0.0))
        acc = acc + _dot(a.astype(BF16), w2_ref[f0:f0 + ff_tile, :])
    o_ref[...] = _rms(acc, gf_ref[...]) if final_norm else acc


def _mlp(x2d, g, w1, w2, gf, tm, final_norm):
    t, d = x2d.shape
    row = pl.BlockSpec((tm, d), lambda i: (i, 0))
    return pl.pallas_call(
        functools.partial(_mlp_kernel, ff_tile=min(1024, w1.shape[1]), final_norm=final_norm),
        grid=(t // tm,),
        in_specs=[row, _full(g.shape), _full(w1.shape), _full(w2.shape), _full(gf.shape)],
        out_specs=row,
        out_shape=jax.ShapeDtypeStruct((t, d), F32),
        compiler_params=_params(("parallel",)),
        name="mlp",
    )(x2d, g, w1, w2, gf)


def _slot_cols(w, heads, hd):
    k = w.shape[0]
    w = w.reshape(k, heads, hd)
    return jnp.pad(w, ((0, 0), (0, 0), (0, LANES - hd))).reshape(k, heads * LANES)


REPACK_PIECES = 3


def _repack_kernel(blk_ref, par_ref, a_ref, b_ref, o_ref, *, n_cols):
    i = pl.program_id(0)
    n_layers, k = a_ref.shape[1], a_ref.shape[2]
    row = blk_ref[i] * LANES + lax.broadcasted_iota(jnp.int32, (2 * LANES, k), 0)
    j = lax.broadcasted_iota(jnp.int32, (LANES, 2 * LANES), 0)
    r = lax.broadcasted_iota(jnp.int32, (LANES, 2 * LANES), 1)
    sel = None
    for p in range(REPACK_PIECES):
        off = par_ref[(i * REPACK_PIECES + p) * 3]
        width = par_ref[(i * REPACK_PIECES + p) * 3 + 1]
        dst = par_ref[(i * REPACK_PIECES + p) * 3 + 2]
        hit = (r - off == j - dst) & (j >= dst) & (j < dst + width)
        sel = hit if sel is None else sel | hit
    sel = sel.astype(BF16)
    for l in range(n_layers):
        window = jnp.concatenate([a_ref[:, l, :], b_ref[:, l, :]], axis=0)
        window = jnp.where(row < n_cols, window, 0.0).astype(BF16)
        o_ref[l] = _dot(sel, window).T.astype(BF16)


def _repack(w_all, blocks):
    n_layers, k, n_cols = w_all.shape
    last_blk = (n_cols - 1) // LANES
    blk, par = [], []
    for pieces in blocks:
        b0 = min(p[0] for p in pieces) // LANES
        assert len(pieces) <= REPACK_PIECES
        pieces = list(pieces) + [(b0 * LANES, 0, 0)] * (REPACK_PIECES - len(pieces))
        for src, width, dst in pieces:
            assert 0 <= src - b0 * LANES and src - b0 * LANES + width <= 2 * LANES and dst + width <= LANES
            par += [src - b0 * LANES, width, dst]
        blk.append(b0)
    w_t = jnp.transpose(w_all, (2, 0, 1))
    grid_spec = pltpu.PrefetchScalarGridSpec(
        num_scalar_prefetch=2,
        grid=(len(blocks),),
        in_specs=[pl.BlockSpec((LANES, n_layers, k), lambda i, blk, par: (blk[i], 0, 0)),
                  pl.BlockSpec((LANES, n_layers, k),
                               lambda i, blk, par: (jnp.minimum(blk[i] + 1, last_blk), 0, 0))],
        out_specs=pl.BlockSpec((n_layers, k, LANES), lambda i, blk, par: (0, 0, i)),
    )
    return pl.pallas_call(
        functools.partial(_repack_kernel, n_cols=n_cols),
        grid_spec=grid_spec,
        out_shape=jax.ShapeDtypeStruct((n_layers, k, len(blocks) * LANES), BF16),
        compiler_params=_params(("arbitrary",)),
        name="repack",
    )(jnp.asarray(np.array(blk, np.int32)), jnp.asarray(np.array(par, np.int32)), w_t, w_t)


def _inproj_weights(w_all):
    d = w_all.shape[1]
    fw, kw, vw = FOX_HEADS * FOX_HD, GLA_HEADS * GLA_DK, GLA_HEADS * GLA_DV
    sizes = (fw, fw, fw, FOX_HEADS, kw, kw, vw, GLA_GATE_RANK, vw,
             MLA_Q_RANK, MLA_KV_RANK, MLA_ROPE, N_BRANCH * d)
    starts = np.concatenate([[0], np.cumsum(sizes)]).tolist()
    fq, fk, fv, ff, gq, gk, gv, glow, gr, mq, mkv, mkr, zg = starts[:-1]
    half = MLA_ROPE // 2
    dense = lambda c0, width: [[(c0 + i, LANES, 0)] for i in range(0, width, LANES)]
    blocks = (dense(fq, fw) + dense(fk, fw) + dense(fv, fw)
              + dense(gq, kw) + dense(gk, kw) + dense(gv, vw) + dense(gr, vw)
              + [[(ff, FOX_HEADS, p * FOX_HEADS) for p in range(N_PIECES)],
                 [(glow, GLA_GATE_RANK, 0)], [(mkr, MLA_ROPE, ROPE_LANE)],
                 [(mkr + half, half, ROPE_LANE), (mkr, half, ROPE_LANE + half)]]
              + dense(mq, MLA_Q_RANK) + dense(mkv, MLA_KV_RANK))
    n, tr = False, True
    outs = [(fw, BF16, n), (fw, BF16, n), (fw, BF16, n), (kw, BF16, n), (kw, F32, n),
            (vw, BF16, tr), (vw, F32, n), (LANES, F32, n), (LANES, F32, n), (LANES, F32, n),
            (LANES, F32, n), (MLA_Q_RANK, F32, n), (MLA_KV_RANK, F32, n)]
    return _repack(w_all, blocks), outs, _repack(w_all, dense(zg, N_BRANCH * d))


def _mla_weights(w_uq, w_ukv):
    half = MLA_ROPE // 2
    qk = MLA_NOPE + MLA_ROPE
    r = w_uq.shape[0]
    wq = w_uq.reshape(r, MLA_HEADS, qk)
    nope, x1, x2 = wq[..., :MLA_NOPE], wq[..., MLA_NOPE:MLA_NOPE + half], wq[..., MLA_NOPE + half:]
    pad = jnp.zeros((r, MLA_HEADS, LANES - qk), F32)
    wqa = jnp.concatenate([nope, x1, x2, pad], axis=-1).reshape(r, SLOTS)
    wqb = jnp.concatenate([jnp.zeros_like(nope), x2, x1, pad], axis=-1).reshape(r, SLOTS)
    rk = w_ukv.shape[0]
    wkv = w_ukv.reshape(rk, MLA_HEADS, MLA_NOPE + MLA_VD)
    wk = _slot_cols(wkv[..., :MLA_NOPE].reshape(rk, -1), MLA_HEADS, MLA_NOPE)
    wv = _slot_cols(wkv[..., MLA_NOPE:].reshape(rk, -1), MLA_HEADS, MLA_VD)
    return wqa.astype(BF16), wqb.astype(BF16), wk.astype(BF16), wv.astype(BF16)


def _rope_tables(seq):
    half = MLA_ROPE // 2
    inv = ROPE_BASE ** (-jnp.arange(half, dtype=F32) / half)
    ang = jnp.arange(seq).astype(F32)[:, None] * inv[None, :]
    cos, sin = jnp.cos(ang), jnp.sin(ang)
    one = jnp.ones((seq, MLA_NOPE), F32)
    zero = jnp.zeros((seq, LANES - MLA_NOPE - MLA_ROPE), F32)
    ct = jnp.concatenate([one, cos, cos, zero], axis=1)
    st = jnp.concatenate([jnp.zeros_like(one), -sin, sin, zero], axis=1)
    return ct, st


def _tile(n, pref):
    t = min(n, pref)
    assert n % t == 0, (n, t)
    return t


def kernel(x, mem, g_mix, w_in, b_fox_forget, w_gla_gate, b_gla_gate, g_gla_out, g_mla_q, w_mla_uq, g_mla_kv, w_mla_ukv, b_branch_gate, w_up_fox, w_up_gla, w_up_mla, w_out, g_xa, g_mem, w_xq, w_xkv, w_xo, g_mlp, w_mlp1, w_mlp2, g_final):
    b, s, d = x.shape
    depth = w_in.shape[0]
    t = b * s
    mlen = mem.shape[1]
    tm = _tile(t, 1024)
    ts = _tile(s, 512)
    tq = _tile(s, 512)
    ct, st = _rope_tables(s)
    x2d = x.reshape(t, d)
    mem2d = mem.reshape(b * mlen, d)
    row = lambda v: v.reshape(1, -1)
    w1_all, outs, wz_all = _inproj_weights(w_in)
    for l in range(depth):
        w1, wz = w1_all[l], wz_all[l]
        fq, fk, fv, gq, gk, gv, gr, ff, glow, kr, krs, cq, ckv = _inproj(x2d, row(g_mix[l]), w1, outs, tm)
        as3 = lambda a: a.reshape(b, s, a.shape[-1])
        bf = jnp.pad(jnp.tile(b_fox_forget[l], N_PIECES), (0, LANES - N_PIECES * FOX_HEADS)).reshape(1, LANES)
        wqa, wqb, wk, wv = _mla_weights(w_mla_uq[l], w_mla_ukv[l])
        (fqt, fka, fvt), (mqt, mk, mvt) = _prep(
            as3(fq), as3(fk), as3(fv), as3(ff), bf, cq, ckv, kr, krs, row(g_mla_q[l]),
            row(g_mla_kv[l]), wqa, wqb, wk, wv, ct, st, ts)
        o_fox = _flash(fqt, fka, fvt, mode="causal", tq=tq)
        o_mla = _flash(mqt, mk, mvt, mode="chunk", tq=tq)
        wg = jnp.pad(w_gla_gate[l], ((0, LANES - GLA_GATE_RANK), (0, 0))).astype(BF16)
        o_gla = _gla(as3(gq), as3(gk), gv, as3(gr), as3(glow), wg, row(b_gla_gate[l]),
                     row(g_gla_out[l]), _tile(s, 1024))
        x2d = _merge(x2d, o_fox.reshape(t, -1), o_gla.reshape(t, -1), o_mla.reshape(t, -1),
                     row(g_mix[l]), wz, row(b_branch_gate[l]), w_up_fox[l].astype(BF16),
                     w_up_gla[l].astype(BF16), w_up_mla[l].astype(BF16), w_out[l].astype(BF16), tm)
        km, vmt = _memkv(mem2d, row(g_mem[l]), w_xkv[l].astype(BF16), _tile(b * mlen, 512))
        x2d = _xattn(x2d.reshape(b, s, d), km.reshape(b, mlen, -1), vmt,
                     row(g_xa[l]), w_xq[l].astype(BF16), w_xo[l].astype(BF16),
                     _tile(s, 1024)).reshape(t, d)
        x2d = _mlp(x2d, row(g_mlp[l]), w_mlp1[l].astype(BF16), w_mlp2[l].astype(BF16),
                   row(g_final), tm, final_norm=(l == depth - 1))
    return x2d.reshape(b, s, d)
```
